```python
import jax, jax.numpy as jnp
from jax import lax
import numpy as np

D_MODEL = 2048
BATCH = 1
SEQ = 8192
DEPTH = 2
DEC_BATCH = 128
DEC_SEQ = 1
PAST_LEN = 16384
PAGE_SIZE = 128

N_EVEN = (DEPTH + 1) // 2
N_ODD = DEPTH // 2
MLA_HEADS = 16
QK_NOPE = 128
QK_ROPE = 64
V_HEAD = 128
KV_LORA = 512
MLA_ROW = KV_LORA + QK_ROPE
MLA_SCALE = (QK_NOPE + QK_ROPE) ** -0.5
ROPE_BASE = 10000.0
RET_HEADS = 8
RET_DK = D_MODEL // 16
RET_DV = D_MODEL // 8
RET_CHUNK = 128
RET_ROT_BASE = 10000.0
POOL_WINDOWS = (2, 4, 8, 16)
POOL_GROUPS = 4
POOL_CH = D_MODEL // POOL_GROUPS
POOL_BUF = 15
FFN = 4 * D_MODEL
PLE_DIM = 256
Q_BLOCK = 128
EPS = 1e-6
NEG_INF = -1e30
IN_SIZES = (MLA_HEADS * (QK_NOPE + QK_ROPE), MLA_ROW, RET_HEADS * RET_DK, RET_HEADS * RET_DK, RET_HEADS * RET_DV, RET_HEADS * RET_DV)
IN_WIDTH = 3072 + 576 + 1024 + 1024 + 2048 + 2048
OUT_WIDTH = MLA_HEADS * V_HEAD + RET_HEADS * RET_DV

kernel_name = 'mla_retention_pool_hybrid_step'


def rmsnorm(x, g):
    xf = x.astype(jnp.float32)
    y = xf * lax.rsqrt(jnp.mean(xf * xf, axis=-1, keepdims=True) + EPS)
    return (y * g.astype(jnp.float32)).astype(x.dtype)


def rope(x, pos, base):
    d = x.shape[-1]
    half = d // 2
    inv = jnp.power(base, -jnp.arange(half, dtype=jnp.float32) / half)
    ang = pos.astype(jnp.float32)[:, None] * inv[None, :]
    shape = (1, pos.shape[0]) + (1,) * (x.ndim - 3) + (half,)
    cos = jnp.cos(ang).reshape(shape)
    sin = jnp.sin(ang).reshape(shape)
    xf = x.astype(jnp.float32)
    x1, x2 = xf[..., :half], xf[..., half:]
    return jnp.concatenate([x1 * cos - x2 * sin, x2 * cos + x1 * sin], axis=-1).astype(x.dtype)


def even_project(xn, pos, w_in, q_gain, kv_gain, kr_gain, w_uk):
    b, t, _ = xn.shape
    z = xn @ w_in
    q, kv, rq, rk, rv, rg = jnp.split(z, np.cumsum(IN_SIZES)[:-1].tolist(), axis=-1)
    q = rmsnorm(q.reshape(b, t, MLA_HEADS, QK_NOPE + QK_ROPE), q_gain)
    q_nope = q[..., :QK_NOPE]
    q_pe = rope(q[..., QK_NOPE:], pos, ROPE_BASE)
    q_lat = jnp.einsum('bthd,chd->bthc', q_nope, w_uk)
    q_full = jnp.concatenate([q_lat, q_pe], axis=-1)
    c = rmsnorm(kv[..., :KV_LORA], kv_gain)
    kr = rope(rmsnorm(kv[..., KV_LORA:], kr_gain), pos, ROPE_BASE)
    rows = jnp.concatenate([c, kr], axis=-1)
    rq = rope(rq.reshape(b, t, RET_HEADS, RET_DK), pos, RET_ROT_BASE)
    rk = rope(rk.reshape(b, t, RET_HEADS, RET_DK), pos, RET_ROT_BASE) * (RET_DK ** -0.5)
    rv = rv.reshape(b, t, RET_HEADS, RET_DV)
    return q_full, rows, rq, rk, rv, rg


def mla_prompt(q_full, rows):
    b, s, h, c = q_full.shape
    nb = s // Q_BLOCK
    qb = q_full.reshape(b, nb, Q_BLOCK, h, c).transpose(1, 0, 2, 3, 4)
    kpos = jnp.arange(s)
    lat = rows[..., :KV_LORA]

    def block(args):
        q_blk, start = args
        sc = jnp.einsum('bqhc,bkc->bhqk', q_blk, rows, preferred_element_type=jnp.float32) * MLA_SCALE
        qpos = start + jnp.arange(Q_BLOCK)
        sc = jnp.where((kpos[None, :] <= qpos[:, None])[None, None], sc, NEG_INF)
        p = jax.nn.softmax(sc, axis=-1)
        return jnp.einsum('bhqk,bkc->bqhc', p.astype(lat.dtype), lat)

    o = lax.map(block, (qb, jnp.arange(nb) * Q_BLOCK))
    return o.transpose(1, 0, 2, 3, 4).reshape(b, s, h, KV_LORA)


def mla_sample(q_full, new_rows, cache_mla, e, page_table):
    t = q_full.shape[1]
    causal = jnp.tril(jnp.ones((t, t), dtype=bool))

    def one(args):
        pt, q, new = args
        past = cache_mla[e, pt].reshape(-1, MLA_ROW)
        n_past = past.shape[0]
        s_past = jnp.einsum('thc,kc->htk', q, past, preferred_element_type=jnp.float32) * MLA_SCALE
        s_new = jnp.einsum('thc,kc->htk', q, new, preferred_element_type=jnp.float32) * MLA_SCALE
        s_new = jnp.where(causal[None], s_new, NEG_INF)
        p = jax.nn.softmax(jnp.concatenate([s_past, s_new], axis=-1), axis=-1)
        o = jnp.einsum('htk,kc->thc', p[..., :n_past].astype(past.dtype), past[:, :KV_LORA])
        o = o + jnp.einsum('htk,kc->thc', p[..., n_past:].astype(new.dtype), new[:, :KV_LORA])
        return o

    return lax.map(one, (page_table, q_full, new_rows))


def retention_chunk(s_prev, qc, kc, vc, log_g):
    f32 = jnp.float32
    qc, kc, vc, s_prev = qc.astype(f32), kc.astype(f32), vc.astype(f32), s_prev.astype(f32)
    c = qc.shape[1]
    idx = jnp.arange(c, dtype=f32)
    diff = idx[:, None] - idx[None, :]
    decay = jnp.where(diff[None] >= 0, jnp.exp(log_g[:, None, None] * jnp.maximum(diff, 0.0)[None]), 0.0)
    sc = jnp.einsum('bihd,bjhd->bhij', qc, kc) * decay[None]
    o_in = jnp.einsum('bhij,bjhe->bihe', sc, vc)
    q_dec = jnp.exp(log_g[None, :] * (idx[:, None] + 1.0))
    o_x = jnp.einsum('bihd,bhde->bihe', qc, s_prev) * q_dec[None, :, :, None]
    k_dec = jnp.exp(log_g[None, :] * (c - 1.0 - idx[:, None]))
    s_new = jnp.exp(log_g * c)[None, :, None, None] * s_prev + jnp.einsum('bjhd,bjhe->bhde', kc * k_dec[None, :, :, None], vc)
    return s_new, o_in + o_x


def retention_prompt(rq, rk, rv, log_g):
    b, s, h, _ = rq.shape
    nc = s // RET_CHUNK

    def chunks(a):
        return a.reshape(b, nc, RET_CHUNK, h, a.shape[-1]).transpose(1, 0, 2, 3, 4)

    s0 = jnp.zeros((b, h, RET_DK, RET_DV), jnp.float32)

    def step(st, inp):
        qc, kc, vc = inp
        return retention_chunk(st, qc, kc, vc, log_g)

    s_fin, o = lax.scan(step, s0, (chunks(rq), chunks(rk), chunks(rv)))
    return s_fin, o.transpose(1, 0, 2, 3, 4).reshape(b, s, h, RET_DV)


def even_output(o_lat, ret_o, rg, w_uv, ret_gn, w_out, dt):
    b, t = o_lat.shape[:2]
    v = jnp.einsum('bthc,chd->bthd', o_lat, w_uv).reshape(b, t, MLA_HEADS * V_HEAD)
    mu = jnp.mean(ret_o, axis=-1, keepdims=True)
    var = jnp.mean(jnp.square(ret_o - mu), axis=-1, keepdims=True)
    r = ((ret_o - mu) * lax.rsqrt(var + EPS)).reshape(b, t, RET_HEADS * RET_DV) * ret_gn.astype(jnp.float32)
    r = jax.nn.silu(rg.astype(jnp.float32)) * r
    cat = jnp.concatenate([v.astype(dt), r.astype(dt)], axis=-1)
    return (cat @ w_out).astype(dt)


def pool_mix(xn, buf, pos, w_grp, scale):
    b, t, d = xn.shape
    ext = jnp.concatenate([buf.astype(xn.dtype), xn], axis=1)
    cs = jnp.cumsum(jnp.concatenate([jnp.zeros((b, 1, d), jnp.float32), ext.astype(jnp.float32)], axis=1), axis=1)
    end = cs[:, POOL_BUF + 1:POOL_BUF + 1 + t]
    means = []
    for g, w in enumerate(POOL_WINDOWS):
        ch = slice(g * POOL_CH, (g + 1) * POOL_CH)
        wsum = end[..., ch] - cs[:, POOL_BUF + 1 - w:POOL_BUF + 1 - w + t, ch]
        cnt = jnp.minimum(w, pos + 1).astype(jnp.float32)[None, :, None]
        means.append(wsum / cnt)
    pooled = jnp.concatenate(means, axis=-1) - xn.astype(jnp.float32)
    y = jnp.einsum('btgc,gcd->btgd', pooled.reshape(b, t, POOL_GROUPS, POOL_CH), w_grp.astype(jnp.float32))
    y = y.reshape(b, t, d) * scale.astype(jnp.float32)
    return y.astype(xn.dtype), ext[:, -POOL_BUF:]


def channel_and_ple(h, p_i, g_mlp, w_up, w_down, g_ple, w_pe, w_pg):
    xn = rmsnorm(h, g_mlp)
    u = jnp.square(jax.nn.relu(xn @ w_up))
    h = h + (u @ w_down).astype(h.dtype)
    gate = jax.nn.sigmoid((rmsnorm(h, g_ple) @ w_pg).astype(jnp.float32))
    return h + ((p_i @ w_pe).astype(jnp.float32) * gate).astype(h.dtype)


def setup_inputs(seed: int = 0) -> dict:
    key = jax.random.key(seed)
    ks = jax.random.split(key, 32)
    n_pages = PAST_LEN // PAGE_SIZE
    n_used = DEC_BATCH * n_pages
    n_pool = n_used + max(1, n_used // 4)
    nrm = jax.random.normal
    f32 = jnp.float32
    page_table = jax.random.permutation(ks[0], n_pool)[:n_used].reshape(DEC_BATCH, n_pages).astype(jnp.int32)
    return {
        'x_prompt': nrm(ks[1], (BATCH, SEQ, D_MODEL), f32),
        'x_sample': nrm(ks[2], (DEC_BATCH, DEC_SEQ, D_MODEL), f32),
        'cache_mla': nrm(ks[3], (N_EVEN, n_pool, PAGE_SIZE, MLA_ROW), f32),
        'state_ret': 0.5 * nrm(ks[4], (N_EVEN, DEC_BATCH, RET_HEADS, RET_DK, RET_DV), f32),
        'state_pool': nrm(ks[5], (N_ODD, DEC_BATCH, POOL_BUF, D_MODEL), f32),
        'page_table': page_table,
        'p_prompt': nrm(ks[6], (DEPTH, BATCH, SEQ, PLE_DIM), f32),
        'p_sample': nrm(ks[7], (DEPTH, DEC_BATCH, DEC_SEQ, PLE_DIM), f32),
        'norm_mix': 1.0 + 0.1 * nrm(ks[8], (DEPTH, D_MODEL), f32),
        'w_in': nrm(ks[9], (N_EVEN, D_MODEL, IN_WIDTH), f32) * D_MODEL ** -0.5,
        'q_gain': 1.0 + 0.1 * nrm(ks[10], (N_EVEN, QK_NOPE + QK_ROPE), f32),
        'kv_gain': 1.0 + 0.1 * nrm(ks[11], (N_EVEN, KV_LORA), f32),
        'kr_gain': 1.0 + 0.1 * nrm(ks[12], (N_EVEN, QK_ROPE), f32),
        'w_uk': nrm(ks[13], (N_EVEN, KV_LORA, MLA_HEADS, QK_NOPE), f32) * KV_LORA ** -0.5,
        'w_uv': nrm(ks[14], (N_EVEN, KV_LORA, MLA_HEADS, V_HEAD), f32) * KV_LORA ** -0.5,
        'ret_gn': 1.0 + 0.1 * nrm(ks[15], (N_EVEN, RET_HEADS * RET_DV), f32),
        'w_out': nrm(ks[16], (N_EVEN, OUT_WIDTH, D_MODEL), f32) * OUT_WIDTH ** -0.5,
        'pool_w': nrm(ks[17], (N_ODD, POOL_GROUPS, POOL_CH, POOL_CH), f32) * POOL_CH ** -0.5,
        'pool_scale': 1.0 + 0.1 * nrm(ks[18], (N_ODD, D_MODEL), f32),
        'norm_mlp': 1.0 + 0.1 * nrm(ks[19], (DEPTH, D_MODEL), f32),
        'w_up': nrm(ks[20], (DEPTH, D_MODEL, FFN), f32) * D_MODEL ** -0.5,
        'w_down': nrm(ks[21], (DEPTH, FFN, D_MODEL), f32) * FFN ** -0.5,
        'norm_ple': 1.0 + 0.1 * nrm(ks[22], (DEPTH, D_MODEL), f32),
        'w_pe': nrm(ks[23], (DEPTH, PLE_DIM, D_MODEL), f32) * PLE_DIM ** -0.5,
        'w_pg': nrm(ks[24], (DEPTH, D_MODEL, D_MODEL), f32) * D_MODEL ** -0.5,
    }


def reference(x_prompt, x_sample, cache_mla, state_ret, state_pool, page_table, p_prompt, p_sample,
              norm_mix, w_in, q_gain, kv_gain, kr_gain, w_uk, w_uv, ret_gn, w_out, pool_w, pool_scale,
              norm_mlp, w_up, w_down, norm_ple, w_pe, w_pg):
    past_len = page_table.shape[1] * cache_mla.shape[2]
    pos_p = jnp.arange(x_prompt.shape[1])
    pos_s = past_len + jnp.arange(x_sample.shape[1])
    log_g = jnp.log1p(-jnp.power(2.0, -5.0 - jnp.arange(RET_HEADS, dtype=jnp.float32)))
    hp, hs = x_prompt, x_sample
    mla_p, mla_s, ret_p, ret_s, pool_p, pool_s = [], [], [], [], [], []
    for i in range(DEPTH):
        if i % 2 == 0:
            e = i // 2
            xn = rmsnorm(hp, norm_mix[i])
            qf, rows, rq, rk, rv, rg = even_project(xn, pos_p, w_in[e], q_gain[e], kv_gain[e], kr_gain[e], w_uk[e])
            o_lat = mla_prompt(qf, rows)
            s_fin, ro = retention_prompt(rq, rk, rv, log_g)
            hp = hp + even_output(o_lat, ro, rg, w_uv[e], ret_gn[e], w_out[e], hp.dtype)
            mla_p.append(rows)
            ret_p.append(s_fin)
            xn = rmsnorm(hs, norm_mix[i])
            qf, rows, rq, rk, rv, rg = even_project(xn, pos_s, w_in[e], q_gain[e], kv_gain[e], kr_gain[e], w_uk[e])
            o_lat = mla_sample(qf, rows, cache_mla, e, page_table)
            s_new, ro = retention_chunk(state_ret[e], rq, rk, rv, log_g)
            hs = hs + even_output(o_lat, ro, rg, w_uv[e], ret_gn[e], w_out[e], hs.dtype)
            mla_s.append(rows)
            ret_s.append(s_new)
        else:
            o = i // 2
            xn = rmsnorm(hp, norm_mix[i])
            y, buf = pool_mix(xn, jnp.zeros((hp.shape[0], POOL_BUF, D_MODEL), xn.dtype), pos_p, pool_w[o], pool_scale[o])
            hp = hp + y
            pool_p.append(buf)
            xn = rmsnorm(hs, norm_mix[i])
            y, buf = pool_mix(xn, state_pool[o], pos_s, pool_w[o], pool_scale[o])
            hs = hs + y
            pool_s.append(buf)
        hp = channel_and_ple(hp, p_prompt[i], norm_mlp[i], w_up[i], w_down[i], norm_ple[i], w_pe[i], w_pg[i])
        hs = channel_and_ple(hs, p_sample[i], norm_mlp[i], w_up[i], w_down[i], norm_ple[i], w_pe[i], w_pg[i])
    return (hp, hs, jnp.stack(mla_p), jnp.stack(mla_s), jnp.stack(ret_p), jnp.stack(ret_s), jnp.stack(pool_p), jnp.stack(pool_s))
```

```python
import functools

import jax
import jax.numpy as jnp
from jax import lax
from jax.experimental import pallas as pl
from jax.experimental.pallas import tpu as pltpu

F32 = jnp.float32
BF16 = jnp.bfloat16

EPS = 1e-6
NEG_INF = -1e30
MLA_HEADS = 16
QK_NOPE = 128
QK_ROPE = 64
QK_HEAD = QK_NOPE + QK_ROPE
V_HEAD = 128
KV_LORA = 512
MLA_ROW = KV_LORA + QK_ROPE
MLA_SCALE = QK_HEAD ** -0.5
ROPE_BASE = 10000.0
RET_HEADS = 8
RET_DK = 128
RET_DV = 256
RET_CHUNK = 128
RET_ROT_BASE = 10000.0
POOL_WINDOWS = (2, 4, 8, 16)
POOL_GROUPS = 4
POOL_BUF = 15
POOL_HALO = 16
Q_BLOCK = 128

LANES = 128
VMEM_LIMIT = 48 * 1024 * 1024

NT_DIMS = (((1,), (1,)), ((), ()))


def _params(*sem):
    return pltpu.CompilerParams(dimension_semantics=sem, vmem_limit_bytes=VMEM_LIMIT)


def _sigmoid(x):
    return 1.0 / (1.0 + jnp.exp(-x))


def _rms_scale(x, width):
    return lax.rsqrt(jnp.sum(x * x, axis=-1, keepdims=True) / width + EPS)


def _rmsnorm_body(x_ref, g_ref, o_ref):
    x = x_ref[...]
    r = lax.rsqrt(jnp.mean(x * x, axis=-1, keepdims=True) + EPS)
    o_ref[...] = ((x * r) * g_ref[...]).astype(o_ref.dtype)


def rmsnorm_bf16(x, g):
    m, d = x.shape
    tm = min(m, 512)
    return pl.pallas_call(
        _rmsnorm_body,
        grid=(m // tm,),
        in_specs=[pl.BlockSpec((tm, d), lambda i: (i, 0)),
                  pl.BlockSpec((1, d), lambda i: (0, 0))],
        out_specs=pl.BlockSpec((tm, d), lambda i: (i, 0)),
        out_shape=jax.ShapeDtypeStruct((m, d), BF16),
        compiler_params=_params("parallel"),
        name="rmsnorm",
    )(x, g.reshape(1, d))


def _ep_plain(acc):
    return acc


def _ep_relu2(acc):
    return jnp.square(jnp.maximum(acc, 0.0))


def _ep_residual(acc, res_ref):
    return res_ref[...] + acc


def _ep_ple(acc, h_ref, p_ref, wpe_ref):
    emb = jnp.dot(p_ref[...].astype(BF16), wpe_ref[...], preferred_element_type=F32)
    return h_ref[...] + emb * _sigmoid(acc)


def _mm_body(*refs, nk, n_extra, epilogue):
    a_ref, w_ref = refs[0], refs[1]
    extra = refs[2:2 + n_extra]
    o_ref = refs[2 + n_extra]
    if nk == 1:
        acc = jnp.dot(a_ref[...], w_ref[...], preferred_element_type=F32)
        o_ref[...] = epilogue(acc, *extra).astype(o_ref.dtype)
        return
    acc_ref = refs[3 + n_extra]
    k = pl.program_id(2)

    @pl.when(k == 0)
    def _():
        acc_ref[...] = jnp.zeros_like(acc_ref)

    acc_ref[...] += jnp.dot(a_ref[...], w_ref[...], preferred_element_type=F32)

    @pl.when(k == nk - 1)
    def _():
        o_ref[...] = epilogue(acc_ref[...], *extra).astype(o_ref.dtype)


def matmul(a, w, *, out_dtype, epilogue=_ep_plain, extra=(), extra_specs=None, name="matmul"):
    m, kdim = a.shape
    n = w.shape[1]
    tm = min(m, 1024)
    tn = min(n, 1024)
    tk = min(kdim, 2048)
    nk = kdim // tk
    specs = [pl.BlockSpec((tm, tk), lambda i, j, k: (i, k)),
             pl.BlockSpec((tk, tn), lambda i, j, k: (k, j))]
    if extra_specs is not None:
        specs += extra_specs(tm, tn)
    scratch = [pltpu.VMEM((tm, tn), F32)] if nk > 1 else []
    return pl.pallas_call(
        functools.partial(_mm_body, nk=nk, n_extra=len(extra), epilogue=epilogue),
        grid=(m // tm, n // tn, nk),
        in_specs=specs,
        out_specs=pl.BlockSpec((tm, tn), lambda i, j, k: (i, j)),
        out_shape=jax.ShapeDtypeStruct((m, n), out_dtype),
        scratch_shapes=scratch,
        compiler_params=_params("parallel", "parallel", "arbitrary"),
        name=name,
    )(a, w, *extra)


def _residual_specs(tm, tn):
    return [pl.BlockSpec((tm, tn), lambda i, j, k: (i, j))]


def _ple_specs(ple_dim):
    def specs(tm, tn):
        return [pl.BlockSpec((tm, tn), lambda i, j, k: (i, j)),
                pl.BlockSpec((tm, ple_dim), lambda i, j, k: (i, 0)),
                pl.BlockSpec((ple_dim, tn), lambda i, j, k: (0, j))]
    return specs


def _rotary_pairs(n1, tab):
    t = n1 * tab
    return t + pltpu.roll(t, 64, axis=1)


def _qheads_body(x_ref, w_ref, g_ref, tab_ref, wuk_ref, o_ref):
    acc = jnp.dot(x_ref[...], w_ref[...], preferred_element_type=F32)
    a0 = acc[:, :QK_NOPE]
    a1 = acc[:, QK_NOPE:]
    lane = lax.broadcasted_iota(jnp.int32, a1.shape, 1)
    ssq = (jnp.sum(a0 * a0, axis=-1, keepdims=True)
           + jnp.sum(jnp.where(lane < QK_ROPE, a1 * a1, 0.0), axis=-1, keepdims=True))
    r = lax.rsqrt(ssq / QK_HEAD + EPS)
    n0 = (a0 * r) * g_ref[:, :QK_NOPE]
    n1 = (a1 * r) * g_ref[:, QK_NOPE:]
    pe = _rotary_pairs(n1, tab_ref[...])
    q_lat = jnp.dot(n0.astype(BF16), wuk_ref[...], preferred_element_type=F32)
    o_ref[:, :KV_LORA] = q_lat.astype(o_ref.dtype)
    o_ref[:, KV_LORA:] = pe[:, :QK_ROPE].astype(o_ref.dtype)


def q_heads(xn, wq, gq, tab, wuk_t):
    m, d = xn.shape
    tm = min(m, 512)
    return pl.pallas_call(
        _qheads_body,
        grid=(m // tm, MLA_HEADS),
        in_specs=[pl.BlockSpec((tm, d), lambda i, h: (i, 0)),
                  pl.BlockSpec((None, d, 2 * LANES), lambda i, h: (h, 0, 0)),
                  pl.BlockSpec((1, 2 * LANES), lambda i, h: (0, 0)),
                  pl.BlockSpec((tm, LANES), lambda i, h: (i, 0)),
                  pl.BlockSpec((None, QK_NOPE, KV_LORA), lambda i, h: (h, 0, 0))],
        out_specs=pl.BlockSpec((None, tm, MLA_ROW), lambda i, h: (h, i, 0)),
        out_shape=jax.ShapeDtypeStruct((MLA_HEADS, m, MLA_ROW), BF16),
        compiler_params=_params("parallel", "arbitrary"),
        name="q_heads",
    )(xn, wq, gq, tab, wuk_t)


def _kvrows_body(x_ref, w_ref, gc_ref, gr_ref, tab_ref, rows_ref, rows_bf_ref):
    acc = jnp.dot(x_ref[...], w_ref[...], preferred_element_type=F32)
    ac = acc[:, :KV_LORA]
    c = (ac * _rms_scale(ac, KV_LORA)) * gc_ref[...]
    a1 = acc[:, KV_LORA:]
    lane = lax.broadcasted_iota(jnp.int32, a1.shape, 1)
    ssq = jnp.sum(jnp.where(lane < QK_ROPE, a1 * a1, 0.0), axis=-1, keepdims=True)
    n1 = (a1 * lax.rsqrt(ssq / QK_ROPE + EPS)) * gr_ref[...]
    kr = _rotary_pairs(n1, tab_ref[...])[:, :QK_ROPE]
    rows_ref[:, :KV_LORA] = c
    rows_ref[:, KV_LORA:] = kr
    rows_bf_ref[:, :KV_LORA] = c.astype(BF16)
    rows_bf_ref[:, KV_LORA:] = kr.astype(BF16)


def kv_rows(xn, wkv, gc, gr, tab):
    m, d = xn.shape
    tm = min(m, 512)
    width = KV_LORA + LANES
    return pl.pallas_call(
        _kvrows_body,
        grid=(m // tm,),
        in_specs=[pl.BlockSpec((tm, d), lambda i: (i, 0)),
                  pl.BlockSpec((d, width), lambda i: (0, 0)),
                  pl.BlockSpec((1, KV_LORA), lambda i: (0, 0)),
                  pl.BlockSpec((1, LANES), lambda i: (0, 0)),
                  pl.BlockSpec((tm, LANES), lambda i: (i, 0))],
        out_specs=[pl.BlockSpec((tm, MLA_ROW), lambda i: (i, 0)),
                   pl.BlockSpec((tm, MLA_ROW), lambda i: (i, 0))],
        out_shape=[jax.ShapeDtypeStruct((m, MLA_ROW), F32),
                   jax.ShapeDtypeStruct((m, MLA_ROW), BF16)],
        compiler_params=_params("parallel"),
        name="kv_rows",
    )(xn, wkv, gc, gr, tab)


def _attn_body(q_ref, k_ref, wuv_ref, o_ref, m_scr, l_scr, acc_scr, *, tq, tk):
    i = pl.program_id(0)
    rows = MLA_HEADS * tq
    q = q_ref[...].reshape(rows, MLA_ROW)
    m_scr[...] = jnp.full(m_scr.shape, NEG_INF, F32)
    l_scr[...] = jnp.zeros(l_scr.shape, F32)
    acc_scr[...] = jnp.zeros(acc_scr.shape, F32)

    def step(j, masked):
        start = pl.multiple_of(j * tk, tk)
        k = k_ref[pl.ds(start, tk), :]
        s = lax.dot_general(q, k, NT_DIMS, preferred_element_type=F32) * MLA_SCALE
        if masked:
            qpos = i * tq + lax.broadcasted_iota(jnp.int32, (MLA_HEADS, tq, tk), 1).reshape(rows, tk)
            kpos = start + lax.broadcasted_iota(jnp.int32, (rows, tk), 1)
            s = jnp.where(kpos <= qpos, s, NEG_INF)
        m_prev = m_scr[...]
        m_new = jnp.maximum(m_prev, jnp.max(s, axis=-1, keepdims=True))
        alpha = jnp.exp(m_prev - m_new)
        p = jnp.exp(s - m_new)
        l_scr[...] = alpha * l_scr[...] + jnp.sum(p, axis=-1, keepdims=True)
        pv = jnp.dot(p.astype(BF16), k[:, :KV_LORA], preferred_element_type=F32)
        acc_scr[...] = alpha * acc_scr[...] + pv
        m_scr[...] = m_new

    n_full = (i * tq) // tk

    def loop_body(j, carry):
        step(j, False)
        return carry

    lax.fori_loop(0, n_full, loop_body, 0)
    step(n_full, True)

    for h in range(MLA_HEADS):
        rs = slice(h * tq, (h + 1) * tq)
        o_h = (acc_scr[rs, :] / l_scr[rs, :]).astype(BF16)
        v_h = jnp.dot(o_h, wuv_ref[h], preferred_element_type=F32)
        o_ref[:, h * V_HEAD:(h + 1) * V_HEAD] = v_h.astype(o_ref.dtype)


def mla_prompt_attn(qf, rows_bf, wuv_t):
    s = rows_bf.shape[0]
    tq = Q_BLOCK
    tk = min(s, 512)
    rows = MLA_HEADS * tq
    return pl.pallas_call(
        functools.partial(_attn_body, tq=tq, tk=tk),
        grid=(s // tq,),
        in_specs=[pl.BlockSpec((MLA_HEADS, tq, MLA_ROW), lambda i: (0, i, 0)),
                  pl.BlockSpec((s, MLA_ROW), lambda i: (0, 0)),
                  pl.BlockSpec((MLA_HEADS, KV_LORA, V_HEAD), lambda i: (0, 0, 0))],
        out_specs=pl.BlockSpec((tq, MLA_HEADS * V_HEAD), lambda i: (i, 0)),
        out_shape=jax.ShapeDtypeStruct((s, MLA_HEADS * V_HEAD), BF16),
        scratch_shapes=[pltpu.VMEM((rows, 1), F32),
                        pltpu.VMEM((rows, 1), F32),
                        pltpu.VMEM((rows, KV_LORA), F32)],
        compiler_params=_params("parallel"),
        name="mla_prompt_attn",
    )(qf, rows_bf, wuv_t)


def _decode_body(pt_ref, q_ref, new_ref, *refs, n_pages, page):
    page_refs = refs[:n_pages]
    o_ref = refs[n_pages]
    kbuf, m_scr, l_scr, acc_scr = refs[n_pages + 1:]
    j = pl.program_id(1)
    q = q_ref[...]

    @pl.when(j == 0)
    def _():
        new = new_ref[...]
        s_new = jnp.sum(q.astype(F32) * new, axis=-1, keepdims=True) * MLA_SCALE
        m_scr[...] = s_new
        l_scr[...] = jnp.ones(l_scr.shape, F32)
        acc_scr[...] = jnp.broadcast_to(new[:, :KV_LORA], acc_scr.shape)

    for i in range(n_pages):
        kbuf[i * page:(i + 1) * page, :] = page_refs[i][...].astype(BF16)
    k = kbuf[...]
    s = lax.dot_general(q, k, NT_DIMS, preferred_element_type=F32) * MLA_SCALE
    m_prev = m_scr[...]
    m_new = jnp.maximum(m_prev, jnp.max(s, axis=-1, keepdims=True))
    alpha = jnp.exp(m_prev - m_new)
    p = jnp.exp(s - m_new)
    l_scr[...] = alpha * l_scr[...] + jnp.sum(p, axis=-1, keepdims=True)
    pv = jnp.dot(p.astype(BF16), kbuf[:, :KV_LORA], preferred_element_type=F32)
    acc_scr[...] = alpha * acc_scr[...] + pv
    m_scr[...] = m_new

    @pl.when(j == pl.num_programs(1) - 1)
    def _():
        o_ref[...] = (acc_scr[...] / l_scr[...]).astype(o_ref.dtype)


def mla_decode_attn(page_table, q_s, new_rows, cache, e):
    b, n_tab = page_table.shape
    page = cache.shape[2]
    n_pages = min(n_tab, 16)
    steps = n_tab // n_pages

    def page_spec(i):
        def index(bi, j, pt):
            return (e, pt[bi * n_tab + j * n_pages + i], 0, 0)
        return pl.BlockSpec((None, None, page, MLA_ROW), index)

    grid_spec = pltpu.PrefetchScalarGridSpec(
        num_scalar_prefetch=1,
        grid=(b, steps),
        in_specs=[pl.BlockSpec((None, MLA_HEADS, MLA_ROW), lambda bi, j, pt: (bi, 0, 0)),
                  pl.BlockSpec((None, 1, MLA_ROW), lambda bi, j, pt: (bi, 0, 0))]
                 + [page_spec(i) for i in range(n_pages)],
        out_specs=pl.BlockSpec((None, MLA_HEADS, KV_LORA), lambda bi, j, pt: (bi, 0, 0)),
        scratch_shapes=[pltpu.VMEM((n_pages * page, MLA_ROW), BF16),
                        pltpu.VMEM((MLA_HEADS, 1), F32),
                        pltpu.VMEM((MLA_HEADS, 1), F32),
                        pltpu.VMEM((MLA_HEADS, KV_LORA), F32)],
    )
    return pl.pallas_call(
        functools.partial(_decode_body, n_pages=n_pages, page=page),
        grid_spec=grid_spec,
        out_shape=jax.ShapeDtypeStruct((b, MLA_HEADS, KV_LORA), BF16),
        compiler_params=_params("parallel", "arbitrary"),
        name="mla_decode_attn",
    )(page_table.reshape(-1), q_s, new_rows, *([cache] * n_pages))


def _uv_body(o_ref, w_ref, v_ref):
    v_ref[...] = jnp.dot(o_ref[...], w_ref[...], preferred_element_type=F32).astype(v_ref.dtype)


def uv_project(o_lat_t, wuv_t):
    _, b, _ = o_lat_t.shape
    return pl.pallas_call(
        _uv_body,
        grid=(MLA_HEADS,),
        in_specs=[pl.BlockSpec((None, b, KV_LORA), lambda h: (h, 0, 0)),
                  pl.BlockSpec((None, KV_LORA, V_HEAD), lambda h: (h, 0, 0))],
        out_specs=pl.BlockSpec((b, V_HEAD), lambda h: (0, h)),
        out_shape=jax.ShapeDtypeStruct((b, MLA_HEADS * V_HEAD), BF16),
        compiler_params=_params("parallel"),
        name="uv_project",
    )(o_lat_t, wuv_t)


def _rope_full(x, cc, ss):
    return x * cc + pltpu.roll(x, RET_DK // 2, axis=1) * ss


def _group_norm_gate(ro, gn, gate):
    mu = jnp.mean(ro, axis=-1, keepdims=True)
    d = ro - mu
    var = jnp.mean(d * d, axis=-1, keepdims=True)
    r = (d * lax.rsqrt(var + EPS)) * gn
    return (gate * _sigmoid(gate)) * r


def _ret_prompt_body(logg_ref, rq_ref, rk_ref, rv_ref, rg_ref, cc_ref, ss_ref, gn_ref,
                     o_ref, st_ref, s_scr):
    c = pl.program_id(0)
    n = RET_CHUNK

    @pl.when(c == 0)
    def _():
        s_scr[...] = jnp.zeros(s_scr.shape, F32)

    ii = lax.broadcasted_iota(jnp.int32, (n, n), 0)
    jj = lax.broadcasted_iota(jnp.int32, (n, n), 1)
    diff = (ii - jj).astype(F32)
    col = lax.broadcasted_iota(jnp.int32, (n, 1), 0).astype(F32)
    cc = cc_ref[...]
    ss = ss_ref[...]
    for h in range(RET_HEADS):
        lg = logg_ref[h]
        decay = jnp.where(diff >= 0, jnp.exp(lg * jnp.maximum(diff, 0.0)), 0.0)
        q_dec = jnp.exp(lg * (col + 1.0))
        k_dec = jnp.exp(lg * (n - 1.0 - col))
        g_chunk = jnp.exp(jnp.full((1, 1), lg * n, F32))
        ks = slice(h * RET_DK, (h + 1) * RET_DK)
        vs = slice(h * RET_DV, (h + 1) * RET_DV)
        q = _rope_full(rq_ref[:, ks], cc, ss)
        k = _rope_full(rk_ref[:, ks], cc, ss) * (RET_DK ** -0.5)
        v = rv_ref[:, vs].astype(BF16)
        qb = q.astype(BF16)
        sc = lax.dot_general(qb, k.astype(BF16), NT_DIMS, preferred_element_type=F32) * decay
        o_in = jnp.dot(sc.astype(BF16), v, preferred_element_type=F32)
        s_prev = s_scr[h]
        o_x = jnp.dot(qb, s_prev.astype(BF16), preferred_element_type=F32) * q_dec
        kd_t = (k * k_dec).T.astype(BF16)
        s_scr[h] = g_chunk * s_prev + jnp.dot(kd_t, v, preferred_element_type=F32)
        r = _group_norm_gate(o_in + o_x, gn_ref[:, vs], rg_ref[:, vs])
        o_ref[:, vs] = r.astype(o_ref.dtype)

    @pl.when(c == pl.num_programs(0) - 1)
    def _():
        st_ref[...] = s_scr[...]


def retention_prompt(z_ret, cc, ss, log_g, gn):
    s = z_ret.shape[0]
    n = RET_CHUNK
    qk_w = RET_HEADS * RET_DK
    v_w = RET_HEADS * RET_DV
    return pl.pallas_call(
        _ret_prompt_body,
        grid=(s // n,),
        in_specs=[pl.BlockSpec(memory_space=pltpu.SMEM),
                  pl.BlockSpec((n, qk_w), lambda c: (c, 0)),
                  pl.BlockSpec((n, qk_w), lambda c: (c, 1)),
                  pl.BlockSpec((n, v_w), lambda c: (c, 1)),
                  pl.BlockSpec((n, v_w), lambda c: (c, 2)),
                  pl.BlockSpec((n, LANES), lambda c: (c, 0)),
                  pl.BlockSpec((n, LANES), lambda c: (c, 0)),
                  pl.BlockSpec((1, v_w), lambda c: (0, 0))],
        out_specs=[pl.BlockSpec((n, v_w), lambda c: (c, 0)),
                   pl.BlockSpec((RET_HEADS, RET_DK, RET_DV), lambda c: (0, 0, 0))],
        out_shape=[jax.ShapeDtypeStruct((s, v_w), BF16),
                   jax.ShapeDtypeStruct((RET_HEADS, RET_DK, RET_DV), F32)],
        scratch_shapes=[pltpu.VMEM((RET_HEADS, RET_DK, RET_DV), F32)],
        compiler_params=_params("arbitrary"),
        name="retention_prompt",
    )(log_g, z_ret, z_ret, z_ret, z_ret, cc, ss, gn)


def _ret_sample_body(logg_ref, rq_ref, rk_ref, rv_ref, rg_ref, cc_ref, ss_ref, gn_ref, st_ref,
                     o_ref, so_ref, o_scr, *, tb):
    cc = cc_ref[...]
    ss = ss_ref[...]
    pad = jnp.zeros((LANES - tb, RET_DK), F32)
    for h in range(RET_HEADS):
        g1 = jnp.exp(jnp.full((1, 1), logg_ref[h], F32))
        ks = slice(h * RET_DK, (h + 1) * RET_DK)
        vs = slice(h * RET_DV, (h + 1) * RET_DV)
        q = _rope_full(rq_ref[:, ks], cc, ss)
        k = _rope_full(rk_ref[:, ks], cc, ss) * (RET_DK ** -0.5)
        q_t = jnp.concatenate([q, pad], axis=0).T
        k_t = jnp.concatenate([k, pad], axis=0).T
        v = rv_ref[:, vs]
        for b in range(tb):
            s_new = g1 * st_ref[b, h] + k_t[:, b:b + 1] * v[b:b + 1, :]
            so_ref[b, h] = s_new
            o_scr[b:b + 1, vs] = jnp.sum(q_t[:, b:b + 1] * s_new, axis=0, keepdims=True)
    for h in range(RET_HEADS):
        vs = slice(h * RET_DV, (h + 1) * RET_DV)
        r = _group_norm_gate(o_scr[:, vs], gn_ref[:, vs], rg_ref[:, vs])
        o_ref[:, vs] = r.astype(o_ref.dtype)


def retention_sample(z_ret, cc, ss, log_g, gn, state):
    b = z_ret.shape[0]
    tb = 8
    qk_w = RET_HEADS * RET_DK
    v_w = RET_HEADS * RET_DV
    st_block = (tb, RET_HEADS, RET_DK, RET_DV)
    return pl.pallas_call(
        functools.partial(_ret_sample_body, tb=tb),
        grid=(b // tb,),
        in_specs=[pl.BlockSpec(memory_space=pltpu.SMEM),
                  pl.BlockSpec((tb, qk_w), lambda i: (i, 0)),
                  pl.BlockSpec((tb, qk_w), lambda i: (i, 1)),
                  pl.BlockSpec((tb, v_w), lambda i: (i, 1)),
                  pl.BlockSpec((tb, v_w), lambda i: (i, 2)),
                  pl.BlockSpec((tb, LANES), lambda i: (i, 0)),
                  pl.BlockSpec((tb, LANES), lambda i: (i, 0)),
                  pl.BlockSpec((1, v_w), lambda i: (0, 0)),
                  pl.BlockSpec(st_block, lambda i: (i, 0, 0, 0))],
        out_specs=[pl.BlockSpec((tb, v_w), lambda i: (i, 0)),
                   pl.BlockSpec(st_block, lambda i: (i, 0, 0, 0))],
        out_shape=[jax.ShapeDtypeStruct((b, v_w), BF16),
                   jax.ShapeDtypeStruct(state.shape, F32)],
        scratch_shapes=[pltpu.VMEM((tb, v_w), F32)],
        compiler_params=_params("parallel"),
        name="retention_sample",
    )(log_g, z_ret, z_ret, z_ret, z_ret, cc, ss, gn, state)


def _pool_prompt_body(h_ref, halo_ref, g_ref, w_ref, sc_ref, o_ref, tail_ref, ext_scr, *, tm):
    i = pl.program_id(0)
    x = h_ref[...]
    g = g_ref[...]
    xn = (x * lax.rsqrt(jnp.mean(x * x, axis=-1, keepdims=True) + EPS)) * g
    xh = halo_ref[...]
    halo = (xh * lax.rsqrt(jnp.mean(xh * xh, axis=-1, keepdims=True) + EPS)) * g
    ext_scr[0:POOL_HALO, :] = jnp.where(i == 0, 0.0, halo)
    ext_scr[POOL_HALO:POOL_HALO + tm, :] = xn
    pos = i * tm + lax.broadcasted_iota(jnp.int32, (tm, 1), 0)
    ch = x.shape[1] // POOL_GROUPS
    for gi, win in enumerate(POOL_WINDOWS):
        cs = slice(gi * ch, (gi + 1) * ch)
        wsum = xn[:, cs]
        for back in range(1, win):
            wsum = wsum + ext_scr[POOL_HALO - back:POOL_HALO - back + tm, cs]
        cnt = jnp.minimum(win, pos + 1).astype(F32)
        pooled = wsum / cnt - xn[:, cs]
        y = jnp.dot(pooled.astype(BF16), w_ref[gi], preferred_element_type=F32)
        o_ref[:, cs] = x[:, cs] + y * sc_ref[:, cs]
    tail_ref[...] = ext_scr[tm:tm + POOL_HALO, :]


def pool_prompt(h, g, w, scale):
    s, d = h.shape
    tm = min(s, 256)
    ch = d // POOL_GROUPS
    halo_blocks = tm // POOL_HALO
    return pl.pallas_call(
        functools.partial(_pool_prompt_body, tm=tm),
        grid=(s // tm,),
        in_specs=[pl.BlockSpec((tm, d), lambda i: (i, 0)),
                  pl.BlockSpec((POOL_HALO, d), lambda i: (jnp.maximum(i * halo_blocks - 1, 0), 0)),
                  pl.BlockSpec((1, d), lambda i: (0, 0)),
                  pl.BlockSpec((POOL_GROUPS, ch, ch), lambda i: (0, 0, 0)),
                  pl.BlockSpec((1, d), lambda i: (0, 0))],
        out_specs=[pl.BlockSpec((tm, d), lambda i: (i, 0)),
                   pl.BlockSpec((POOL_HALO, d), lambda i: (0, 0))],
        out_shape=[jax.ShapeDtypeStruct((s, d), F32),
                   jax.ShapeDtypeStruct((POOL_HALO, d), F32)],
        scratch_shapes=[pltpu.VMEM((tm + POOL_HALO, d), F32)],
        compiler_params=_params("arbitrary"),
        name="pool_prompt",
    )(h, h, g, w, scale)


def _pool_sample_body(h_ref, st_ref, g_ref, w_ref, sc_ref, o_ref, buf_ref, *, count):
    x = h_ref[...]
    xn = (x * lax.rsqrt(jnp.mean(x * x, axis=-1, keepdims=True) + EPS)) * g_ref[...]
    ch = x.shape[1] // POOL_GROUPS
    for gi, win in enumerate(POOL_WINDOWS):
        cs = slice(gi * ch, (gi + 1) * ch)
        wsum = xn[:, cs]
        for back in range(1, win):
            wsum = wsum + st_ref[POOL_BUF - back, :, cs]
        pooled = wsum / float(count[gi]) - xn[:, cs]
        y = jnp.dot(pooled.astype(BF16), w_ref[gi], preferred_element_type=F32)
        o_ref[:, cs] = x[:, cs] + y * sc_ref[:, cs]
    for r in range(POOL_BUF - 1):
        buf_ref[r] = st_ref[r + 1]
    buf_ref[POOL_BUF - 1] = xn


def pool_sample(h, state_t, g, w, scale, pos):
    b, d = h.shape
    tb = min(b, 32)
    ch = d // POOL_GROUPS
    count = tuple(min(win, pos + 1) for win in POOL_WINDOWS)
    return pl.pallas_call(
        functools.partial(_pool_sample_body, count=count),
        grid=(b // tb,),
        in_specs=[pl.BlockSpec((tb, d), lambda i: (i, 0)),
                  pl.BlockSpec((POOL_BUF, tb, d), lambda i: (0, i, 0)),
                  pl.BlockSpec((1, d), lambda i: (0, 0)),
                  pl.BlockSpec((POOL_GROUPS, ch, ch), lambda i: (0, 0, 0)),
                  pl.BlockSpec((1, d), lambda i: (0, 0))],
        out_specs=[pl.BlockSpec((tb, d), lambda i: (i, 0)),
                   pl.BlockSpec((POOL_BUF, tb, d), lambda i: (0, i, 0))],
        out_shape=[jax.ShapeDtypeStruct((b, d), F32),
                   jax.ShapeDtypeStruct((POOL_BUF, b, d), F32)],
        compiler_params=_params("parallel"),
        name="pool_sample",
    )(h, state_t, g, w, scale)


def _rope_tables(pos, half, base):
    inv = jnp.power(base, -jnp.arange(half, dtype=F32) / half)
    ang = pos.astype(F32)[:, None] * inv[None, :]
    return jnp.cos(ang), jnp.sin(ang)


def _swap_layout(w):
    half = QK_ROPE // 2
    x1, x2 = w[..., :half], w[..., half:]
    return jnp.concatenate([x1, x2, x2, x1], axis=-1)


def _channel_and_ple(h, p_i, g_mlp, w_up, w_down, g_ple, w_pe, w_pg):
    xn = rmsnorm_bf16(h, g_mlp)
    u = matmul(xn, w_up, out_dtype=BF16, epilogue=_ep_relu2, name="mlp_up")
    h = matmul(u, w_down, out_dtype=F32, epilogue=_ep_residual, extra=(h,),
               extra_specs=_residual_specs, name="mlp_down")
    xn = rmsnorm_bf16(h, g_ple)
    return matmul(xn, w_pg, out_dtype=F32, epilogue=_ep_ple, extra=(h, p_i, w_pe),
                  extra_specs=_ple_specs(p_i.shape[1]), name="ple_gate")


def kernel(x_prompt, x_sample, cache_mla, state_ret, state_pool, page_table, p_prompt, p_sample,
           norm_mix, w_in, q_gain, kv_gain, kr_gain, w_uk, w_uv, ret_gn, w_out, pool_w, pool_scale,
           norm_mlp, w_up, w_down, norm_ple, w_pe, w_pg):
    assert x_prompt.shape[0] == 1 and x_sample.shape[1] == 1
    depth = norm_mix.shape[0]
    _, seq, d = x_prompt.shape
    nb = x_sample.shape[0]
    past_len = page_table.shape[1] * cache_mla.shape[2]
    hp = x_prompt.reshape(seq, d)
    hs = x_sample.reshape(nb, d)
    pos_p = jnp.arange(seq)
    pos_s = jnp.full((nb,), past_len, jnp.int32)
    log_g = jnp.log1p(-jnp.power(2.0, -5.0 - jnp.arange(RET_HEADS, dtype=F32)))

    tabs = []
    for pos in (pos_p, pos_s):
        c, s = _rope_tables(pos, QK_ROPE // 2, ROPE_BASE)
        tab_mla = jnp.concatenate([c, c, -s, s], axis=-1)
        c, s = _rope_tables(pos, RET_DK // 2, RET_ROT_BASE)
        tabs.append((tab_mla, jnp.concatenate([c, c], axis=-1), jnp.concatenate([-s, s], axis=-1)))

    q_w = MLA_HEADS * QK_HEAD
    mla_p, mla_s, ret_p, ret_s, pool_p, pool_s = [], [], [], [], [], []
    for i in range(depth):
        if i % 2 == 0:
            e = i // 2
            wq = w_in[e][:, :q_w].reshape(d, MLA_HEADS, QK_HEAD)
            wq = jnp.concatenate([wq[..., :QK_NOPE], _swap_layout(wq[..., QK_NOPE:])], axis=-1)
            wq = wq.transpose(1, 0, 2).astype(BF16)
            gq = jnp.concatenate([q_gain[e][:QK_NOPE], _swap_layout(q_gain[e][QK_NOPE:])]).reshape(1, -1)
            wkv = w_in[e][:, q_w:q_w + MLA_ROW]
            wkv = jnp.concatenate([wkv[:, :KV_LORA], _swap_layout(wkv[:, KV_LORA:])], axis=-1).astype(BF16)
            gkv = kv_gain[e].reshape(1, -1)
            gkr = _swap_layout(kr_gain[e]).reshape(1, -1)
            w_ret = w_in[e][:, q_w + MLA_ROW:].astype(BF16)
            wuk_t = w_uk[e].transpose(1, 2, 0).astype(BF16)
            wuv_t = w_uv[e].transpose(1, 0, 2).astype(BF16)
            gn = ret_gn[e].reshape(1, -1)
            w_o = w_out[e].astype(BF16)

            def project(h, tab):
                xn = rmsnorm_bf16(h, norm_mix[i])
                qf = q_heads(xn, wq, gq, tab[0], wuk_t)
                rows, rows_bf = kv_rows(xn, wkv, gkv, gkr, tab[0])
                z_ret = matmul(xn, w_ret, out_dtype=F32, name="ret_proj")
                return qf, rows, rows_bf, z_ret

            def output(h, v, r):
                cat = jnp.concatenate([v, r], axis=-1)
                return matmul(cat, w_o, out_dtype=F32, epilogue=_ep_residual, extra=(h,),
                              extra_specs=_residual_specs, name="out_proj")

            qf, rows, rows_bf, z_ret = project(hp, tabs[0])
            v = mla_prompt_attn(qf, rows_bf, wuv_t)
            r, s_fin = retention_prompt(z_ret, tabs[0][1], tabs[0][2], log_g, gn)
            hp = output(hp, v, r)
            mla_p.append(rows.reshape(1, seq, MLA_ROW))
            ret_p.append(s_fin[None])

            qf, rows, rows_bf, z_ret = project(hs, tabs[1])
            o_lat = mla_decode_attn(page_table, qf.transpose(1, 0, 2), rows.reshape(nb, 1, MLA_ROW),
                                    cache_mla, e)
            v = uv_project(o_lat.transpose(1, 0, 2), wuv_t)
            r, s_new = retention_sample(z_ret, tabs[1][1], tabs[1][2], log_g, gn, state_ret[e])
            hs = output(hs, v, r)
            mla_s.append(rows.reshape(nb, 1, MLA_ROW))
            ret_s.append(s_new)
        else:
            o = i // 2
            g = norm_mix[i].reshape(1, -1)
            pw = pool_w[o].astype(BF16)
            sc = pool_scale[o].reshape(1, -1)
            hp, tail = pool_prompt(hp, g, pw, sc)
            pool_p.append(tail[POOL_HALO - POOL_BUF:][None])
            hs, buf = pool_sample(hs, state_pool[o].transpose(1, 0, 2), g, pw, sc, past_len)
            pool_s.append(buf.transpose(1, 0, 2))
        wu = w_up[i].astype(BF16)
        wd = w_down[i].astype(BF16)
        wpe = w_pe[i].astype(BF16)
        wpg = w_pg[i].astype(BF16)
        hp = _channel_and_ple(hp, p_prompt[i, 0], norm_mlp[i], wu, wd, norm_ple[i], wpe, wpg)
        hs = _channel_and_ple(hs, p_sample[i, :, 0], norm_mlp[i], wu, wd, norm_ple[i], wpe, wpg)

    return (hp.reshape(1, seq, d), hs.reshape(nb, 1, d),
            jnp.stack(mla_p), jnp.stack(mla_s), jnp.stack(ret_p), jnp.stack(ret_s),
            jnp.stack(pool_p), jnp.stack(pool_s))
```

```python
import functools

import jax
import jax.numpy as jnp
from jax import lax
from jax.experimental import pallas as pl
from jax.experimental.pallas import tpu as pltpu

F32 = jnp.float32
BF16 = jnp.bfloat16

EPS = 1e-6
NEG_INF = -1e30
MLA_HEADS = 16
QK_NOPE = 128
QK_ROPE = 64
QK_HEAD = QK_NOPE + QK_ROPE
V_HEAD = 128
KV_LORA = 512
MLA_ROW = KV_LORA + QK_ROPE
MLA_SCALE = QK_HEAD ** -0.5
ROPE_BASE = 10000.0
RET_HEADS = 8
RET_DK = 128
RET_DV = 256
RET_CHUNK = 128
RET_ROT_BASE = 10000.0
POOL_WINDOWS = (2, 4, 8, 16)
POOL_GROUPS = 4
POOL_BUF = 15
POOL_HALO = 16
Q_BLOCK = 128

LOG2E = 1.4426950408889634
LANES = 128
VMEM_LIMIT = 48 * 1024 * 1024

NT_DIMS = (((1,), (1,)), ((), ()))


def _params(*sem):
    return pltpu.CompilerParams(dimension_semantics=sem, vmem_limit_bytes=VMEM_LIMIT)


def _sigmoid(x):
    return 1.0 / (1.0 + jnp.exp(-x))


def _rms_scale(x, width):
    return lax.rsqrt(jnp.sum(x * x, axis=-1, keepdims=True) / width + EPS)


def _rmsnorm_body(x_ref, g_ref, o_ref):
    x = x_ref[...]
    r = lax.rsqrt(jnp.mean(x * x, axis=-1, keepdims=True) + EPS)
    o_ref[...] = ((x * r) * g_ref[...]).astype(o_ref.dtype)


def rmsnorm_bf16(x, g):
    m, d = x.shape
    tm = min(m, 512)
    return pl.pallas_call(
        _rmsnorm_body,
        grid=(m // tm,),
        in_specs=[pl.BlockSpec((tm, d), lambda i: (i, 0)),
                  pl.BlockSpec((1, d), lambda i: (0, 0))],
        out_specs=pl.BlockSpec((tm, d), lambda i: (i, 0)),
        out_shape=jax.ShapeDtypeStruct((m, d), BF16),
        compiler_params=_params("parallel"),
        name="rmsnorm",
    )(x, g.reshape(1, d))


def _ep_plain(acc):
    return acc


def _ep_relu2(acc):
    return jnp.square(jnp.maximum(acc, 0.0))


def _ep_residual(acc, res_ref):
    return res_ref[...] + acc


def _ep_ple(acc, h_ref, p_ref, wpe_ref):
    emb = jnp.dot(p_ref[...].astype(BF16), wpe_ref[...], preferred_element_type=F32)
    return h_ref[...] + emb * _sigmoid(acc)


def _mm_body(*refs, nk, n_extra, epilogue):
    a_ref, w_ref = refs[0], refs[1]
    extra = refs[2:2 + n_extra]
    o_ref = refs[2 + n_extra]
    if nk == 1:
        acc = jnp.dot(a_ref[...], w_ref[...], preferred_element_type=F32)
        o_ref[...] = epilogue(acc, *extra).astype(o_ref.dtype)
        return
    acc_ref = refs[3 + n_extra]
    k = pl.program_id(2)

    @pl.when(k == 0)
    def _():
        acc_ref[...] = jnp.zeros_like(acc_ref)

    acc_ref[...] += jnp.dot(a_ref[...], w_ref[...], preferred_element_type=F32)

    @pl.when(k == nk - 1)
    def _():
        o_ref[...] = epilogue(acc_ref[...], *extra).astype(o_ref.dtype)


def matmul(a, w, layer, *, out_dtype, epilogue=_ep_plain, extra=(), extra_specs=None, name="matmul"):
    m, kdim = a.shape
    n = w.shape[2]
    tm = min(m, 1024)
    tn = min(n, 1024)
    tk = min(kdim, 2048)
    nk = kdim // tk
    specs = [pl.BlockSpec((tm, tk), lambda i, j, k: (i, k)),
             pl.BlockSpec((None, tk, tn), lambda i, j, k: (layer, k, j))]
    if extra_specs is not None:
        specs += extra_specs(tm, tn)
    scratch = [pltpu.VMEM((tm, tn), F32)] if nk > 1 else []
    return pl.pallas_call(
        functools.partial(_mm_body, nk=nk, n_extra=len(extra), epilogue=epilogue),
        grid=(m // tm, n // tn, nk),
        in_specs=specs,
        out_specs=pl.BlockSpec((tm, tn), lambda i, j, k: (i, j)),
        out_shape=jax.ShapeDtypeStruct((m, n), out_dtype),
        scratch_shapes=scratch,
        compiler_params=_params("parallel", "parallel", "arbitrary"),
        name=name,
    )(a, w, *extra)


def _out_project_body(v_ref, r_ref, wv_ref, wr_ref, res_ref, o_ref):
    acc = jnp.dot(v_ref[...], wv_ref[...], preferred_element_type=F32)
    acc += jnp.dot(r_ref[...], wr_ref[...], preferred_element_type=F32)
    o_ref[...] = res_ref[...] + acc


def out_project(v, r, w, res):
    m, half = v.shape
    n = w.shape[1]
    tm = min(m, 512)
    tn = min(n, 1024)
    return pl.pallas_call(
        _out_project_body,
        grid=(m // tm, n // tn),
        in_specs=[pl.BlockSpec((tm, half), lambda i, j: (i, 0)),
                  pl.BlockSpec((tm, half), lambda i, j: (i, 0)),
                  pl.BlockSpec((half, tn), lambda i, j: (0, j)),
                  pl.BlockSpec((half, tn), lambda i, j: (1, j)),
                  pl.BlockSpec((tm, tn), lambda i, j: (i, j))],
        out_specs=pl.BlockSpec((tm, tn), lambda i, j: (i, j)),
        out_shape=jax.ShapeDtypeStruct((m, n), F32),
        compiler_params=_params("parallel", "parallel"),
        name="out_proj",
    )(v, r, w, w, res)


def _residual_specs(tm, tn):
    return [pl.BlockSpec((tm, tn), lambda i, j, k: (i, j))]


def _ple_specs(ple_dim, layer):
    def specs(tm, tn):
        return [pl.BlockSpec((tm, tn), lambda i, j, k: (i, j)),
                pl.BlockSpec((tm, ple_dim), lambda i, j, k: (i, 0)),
                pl.BlockSpec((None, ple_dim, tn), lambda i, j, k: (layer, 0, j))]
    return specs


def _rotary_pairs(n1, tab):
    t = n1 * tab
    return t + pltpu.roll(t, 64, axis=1)


Q_HEADS_PER_STEP = 2


def _qheads_body(x_ref, w_ref, g_ref, tab_ref, wuk_ref, o_ref):
    acc2 = jnp.dot(x_ref[...], w_ref[...], preferred_element_type=F32)
    for hh in range(Q_HEADS_PER_STEP):
        base = hh * 2 * LANES
        a0 = acc2[:, base:base + QK_NOPE]
        a1 = acc2[:, base + QK_NOPE:base + 2 * LANES]
        lane = lax.broadcasted_iota(jnp.int32, a1.shape, 1)
        ssq = (jnp.sum(a0 * a0, axis=-1, keepdims=True)
               + jnp.sum(jnp.where(lane < QK_ROPE, a1 * a1, 0.0), axis=-1, keepdims=True))
        r = lax.rsqrt(ssq / QK_HEAD + EPS)
        n0 = (a0 * r) * g_ref[:, :QK_NOPE]
        n1 = (a1 * r) * g_ref[:, QK_NOPE:]
        pe = _rotary_pairs(n1, tab_ref[...])
        q_lat = jnp.dot(n0.astype(BF16), wuk_ref[hh], preferred_element_type=F32)
        o_ref[hh, :, :KV_LORA] = q_lat.astype(o_ref.dtype)
        o_ref[hh, :, KV_LORA:] = pe[:, :QK_ROPE].astype(o_ref.dtype)


def q_heads(xn, wq, gq, tab, wuk_t):
    m, d = xn.shape
    tm = min(m, 1024)
    hs = Q_HEADS_PER_STEP
    return pl.pallas_call(
        _qheads_body,
        grid=(m // tm, MLA_HEADS // hs),
        in_specs=[pl.BlockSpec((tm, d), lambda i, h: (i, 0)),
                  pl.BlockSpec((None, d, hs * 2 * LANES), lambda i, h: (h, 0, 0)),
                  pl.BlockSpec((1, 2 * LANES), lambda i, h: (0, 0)),
                  pl.BlockSpec((tm, LANES), lambda i, h: (i, 0)),
                  pl.BlockSpec((hs, QK_NOPE, KV_LORA), lambda i, h: (h, 0, 0))],
        out_specs=pl.BlockSpec((hs, tm, MLA_ROW), lambda i, h: (h, i, 0)),
        out_shape=jax.ShapeDtypeStruct((MLA_HEADS, m, MLA_ROW), BF16),
        compiler_params=_params("parallel", "arbitrary"),
        name="q_heads",
    )(xn, wq, gq, tab, wuk_t)


def _kvrows_body(x_ref, w_ref, gc_ref, gr_ref, tab_ref, rows_ref, rows_bf_ref):
    acc = jnp.dot(x_ref[...], w_ref[...], preferred_element_type=F32)
    ac = acc[:, :KV_LORA]
    c = (ac * _rms_scale(ac, KV_LORA)) * gc_ref[...]
    a1 = acc[:, KV_LORA:]
    lane = lax.broadcasted_iota(jnp.int32, a1.shape, 1)
    ssq = jnp.sum(jnp.where(lane < QK_ROPE, a1 * a1, 0.0), axis=-1, keepdims=True)
    n1 = (a1 * lax.rsqrt(ssq / QK_ROPE + EPS)) * gr_ref[...]
    kr = _rotary_pairs(n1, tab_ref[...])[:, :QK_ROPE]
    rows_ref[:, :KV_LORA] = c
    rows_ref[:, KV_LORA:] = kr
    rows_bf_ref[:, :KV_LORA] = c.astype(BF16)
    rows_bf_ref[:, KV_LORA:] = kr.astype(BF16)


def kv_rows(xn, wkv, gc, gr, tab):
    m, d = xn.shape
    tm = min(m, 512)
    width = KV_LORA + LANES
    return pl.pallas_call(
        _kvrows_body,
        grid=(m // tm,),
        in_specs=[pl.BlockSpec((tm, d), lambda i: (i, 0)),
                  pl.BlockSpec((d, width), lambda i: (0, 0)),
                  pl.BlockSpec((1, KV_LORA), lambda i: (0, 0)),
                  pl.BlockSpec((1, LANES), lambda i: (0, 0)),
                  pl.BlockSpec((tm, LANES), lambda i: (i, 0))],
        out_specs=[pl.BlockSpec((tm, MLA_ROW), lambda i: (i, 0)),
                   pl.BlockSpec((tm, MLA_ROW), lambda i: (i, 0))],
        out_shape=[jax.ShapeDtypeStruct((m, MLA_ROW), F32),
                   jax.ShapeDtypeStruct((m, MLA_ROW), BF16)],
        compiler_params=_params("parallel"),
        name="kv_rows",
    )(xn, wkv, gc, gr, tab)


def _attn_body(q_ref, k_ref, wuv_ref, o_ref, s_a, s_b, m_scr, l_scr, acc_scr, *, tq, tk):
    i = pl.program_id(0)
    rows = MLA_HEADS * tq
    c = MLA_SCALE * LOG2E
    m_scr[...] = jnp.full(m_scr.shape, NEG_INF, F32)
    l_scr[...] = jnp.zeros(l_scr.shape, F32)
    acc_scr[...] = jnp.zeros(acc_scr.shape, F32)

    def scores(j, s_ref):
        start = pl.multiple_of(j * tk, tk)
        q = q_ref[...].reshape(rows, MLA_ROW)
        s_ref[...] = lax.dot_general(q, k_ref[pl.ds(start, tk), :], NT_DIMS,
                                     preferred_element_type=F32)

    def update(j, s_ref, masked):
        start = pl.multiple_of(j * tk, tk)
        s = s_ref[...]
        if masked:
            qpos = i * tq + lax.broadcasted_iota(jnp.int32, (MLA_HEADS, tq, tk), 1).reshape(rows, tk)
            kpos = start + lax.broadcasted_iota(jnp.int32, (rows, tk), 1)
            s = jnp.where(kpos <= qpos, s, NEG_INF)
        m_prev = m_scr[...]
        m_new = jnp.maximum(m_prev, jnp.max(s, axis=-1, keepdims=True))
        alpha = jnp.exp2((m_prev - m_new) * c)
        p = jnp.exp2((s - m_new) * c)
        l_scr[...] = alpha * l_scr[...] + jnp.sum(p, axis=-1, keepdims=True)
        pv = jnp.dot(p.astype(BF16), k_ref[pl.ds(start, tk), :KV_LORA], preferred_element_type=F32)
        acc_scr[...] = alpha * acc_scr[...] + pv
        m_scr[...] = m_new

    n_full = (i * tq) // tk
    scores(0, s_a)

    def pair(t, carry):
        j = 2 * t
        scores(j + 1, s_b)
        update(j, s_a, False)
        scores(j + 2, s_a)
        update(j + 1, s_b, False)
        return carry

    lax.fori_loop(0, n_full // 2, pair, 0)
    odd = n_full % 2 == 1

    @pl.when(odd)
    def _():
        scores(n_full, s_b)
        update(n_full - 1, s_a, False)
        update(n_full, s_b, True)

    @pl.when(jnp.logical_not(odd))
    def _():
        update(n_full, s_a, True)

    for h in range(MLA_HEADS):
        rs = slice(h * tq, (h + 1) * tq)
        o_h = (acc_scr[rs, :] / l_scr[rs, :]).astype(BF16)
        v_h = jnp.dot(o_h, wuv_ref[h], preferred_element_type=F32)
        o_ref[:, h * V_HEAD:(h + 1) * V_HEAD] = v_h.astype(o_ref.dtype)


def mla_prompt_attn(qf, rows_bf, wuv_t):
    s = rows_bf.shape[0]
    tq = Q_BLOCK
    tk = min(s, 512)
    rows = MLA_HEADS * tq
    return pl.pallas_call(
        functools.partial(_attn_body, tq=tq, tk=tk),
        grid=(s // tq,),
        in_specs=[pl.BlockSpec((MLA_HEADS, tq, MLA_ROW), lambda i: (0, i, 0)),
                  pl.BlockSpec((s, MLA_ROW), lambda i: (0, 0), pipeline_mode=pl.Buffered(1)),
                  pl.BlockSpec((MLA_HEADS, KV_LORA, V_HEAD), lambda i: (0, 0, 0),
                               pipeline_mode=pl.Buffered(1))],
        out_specs=pl.BlockSpec((tq, MLA_HEADS * V_HEAD), lambda i: (i, 0)),
        out_shape=jax.ShapeDtypeStruct((s, MLA_HEADS * V_HEAD), BF16),
        scratch_shapes=[pltpu.VMEM((rows, tk), F32),
                        pltpu.VMEM((rows, tk), F32),
                        pltpu.VMEM((rows, 1), F32),
                        pltpu.VMEM((rows, 1), F32),
                        pltpu.VMEM((rows, KV_LORA), F32)],
        compiler_params=_params("parallel"),
        name="mla_prompt_attn",
    )(qf, rows_bf, wuv_t)


DECODE_GROUPS = 1
DECODE_PAGES = 32


def _decode_body(pt_ref, q_ref, new_ref, *refs, n_pages, page):
    page_refs = refs[:n_pages]
    o_ref = refs[n_pages]
    kbufs = refs[n_pages + 1:n_pages + 1 + DECODE_GROUPS]
    m_scr, l_scr, acc_scr = refs[n_pages + 1 + DECODE_GROUPS:]
    per_group = n_pages // DECODE_GROUPS
    j = pl.program_id(1)
    q = q_ref[...]

    @pl.when(j == 0)
    def _():
        new = new_ref[...]
        s_new = jnp.sum(q.astype(F32) * new, axis=-1, keepdims=True) * MLA_SCALE
        m_scr[...] = s_new
        l_scr[...] = jnp.ones(l_scr.shape, F32)
        acc_scr[...] = jnp.broadcast_to(new[:, :KV_LORA], acc_scr.shape)

    for g in range(DECODE_GROUPS):
        kbuf = kbufs[g]
        for i in range(per_group):
            kbuf[:, i * page:(i + 1) * page] = page_refs[g * per_group + i][...].astype(BF16)
        s = jnp.dot(q, kbuf[...], preferred_element_type=F32) * MLA_SCALE
        m_prev = m_scr[...]
        m_new = jnp.maximum(m_prev, jnp.max(s, axis=-1, keepdims=True))
        alpha = jnp.exp(m_prev - m_new)
        p = jnp.exp(s - m_new)
        l_scr[...] = alpha * l_scr[...] + jnp.sum(p, axis=-1, keepdims=True)
        pv = lax.dot_general(p.astype(BF16), kbuf[:KV_LORA, :], NT_DIMS, preferred_element_type=F32)
        acc_scr[...] = alpha * acc_scr[...] + pv
        m_scr[...] = m_new

    @pl.when(j == pl.num_programs(1) - 1)
    def _():
        o_ref[...] = (acc_scr[...] / l_scr[...]).astype(o_ref.dtype)


def mla_decode_attn(page_table, q_s, new_rows, cache_t, e):
    b, n_tab = page_table.shape
    page = cache_t.shape[3]
    n_pages = min(n_tab, DECODE_PAGES)
    assert n_pages % DECODE_GROUPS == 0
    steps = n_tab // n_pages

    def page_spec(i):
        def index(bi, j, pt):
            return (e, pt[bi * n_tab + j * n_pages + i], 0, 0)
        return pl.BlockSpec((None, None, MLA_ROW, page), index)

    grid_spec = pltpu.PrefetchScalarGridSpec(
        num_scalar_prefetch=1,
        grid=(b, steps),
        in_specs=[pl.BlockSpec((None, MLA_HEADS, MLA_ROW), lambda bi, j, pt: (bi, 0, 0)),
                  pl.BlockSpec((None, 1, MLA_ROW), lambda bi, j, pt: (bi, 0, 0))]
                 + [page_spec(i) for i in range(n_pages)],
        out_specs=pl.BlockSpec((None, MLA_HEADS, KV_LORA), lambda bi, j, pt: (bi, 0, 0)),
        scratch_shapes=[pltpu.VMEM((MLA_ROW, n_pages // DECODE_GROUPS * page), BF16)] * DECODE_GROUPS
                       + [pltpu.VMEM((MLA_HEADS, 1), F32),
                        pltpu.VMEM((MLA_HEADS, 1), F32),
                        pltpu.VMEM((MLA_HEADS, KV_LORA), F32)],
    )
    return pl.pallas_call(
        functools.partial(_decode_body, n_pages=n_pages, page=page),
        grid_spec=grid_spec,
        out_shape=jax.ShapeDtypeStruct((b, MLA_HEADS, KV_LORA), BF16),
        compiler_params=_params("parallel", "arbitrary"),
        name="mla_decode_attn",
    )(page_table.reshape(-1), q_s, new_rows, *([cache_t] * n_pages))


def _uv_body(o_ref, w_ref, v_ref):
    v_ref[...] = jnp.dot(o_ref[...], w_ref[...], preferred_element_type=F32).astype(v_ref.dtype)


def uv_project(o_lat_t, wuv_t):
    _, b, _ = o_lat_t.shape
    return pl.pallas_call(
        _uv_body,
        grid=(MLA_HEADS,),
        in_specs=[pl.BlockSpec((None, b, KV_LORA), lambda h: (h, 0, 0)),
                  pl.BlockSpec((None, KV_LORA, V_HEAD), lambda h: (h, 0, 0))],
        out_specs=pl.BlockSpec((b, V_HEAD), lambda h: (0, h)),
        out_shape=jax.ShapeDtypeStruct((b, MLA_HEADS * V_HEAD), BF16),
        compiler_params=_params("parallel"),
        name="uv_project",
    )(o_lat_t, wuv_t)


def _rope_full(x, cc, ss):
    return x * cc + pltpu.roll(x, RET_DK // 2, axis=1) * ss


def _group_norm_gate(ro, gn, gate):
    mu = jnp.mean(ro, axis=-1, keepdims=True)
    d = ro - mu
    var = jnp.mean(d * d, axis=-1, keepdims=True)
    r = (d * lax.rsqrt(var + EPS)) * gn
    return (gate * _sigmoid(gate)) * r


def _ret_prompt_body(logg_ref, rq_ref, rk_ref, rv_ref, rg_ref, cc_ref, ss_ref, gn_ref,
                     o_ref, st_ref, s_scr):
    c = pl.program_id(0)
    n = RET_CHUNK

    @pl.when(c == 0)
    def _():
        s_scr[...] = jnp.zeros(s_scr.shape, F32)

    ii = lax.broadcasted_iota(jnp.int32, (n, n), 0)
    jj = lax.broadcasted_iota(jnp.int32, (n, n), 1)
    diff = (ii - jj).astype(F32)
    col = lax.broadcasted_iota(jnp.int32, (n, 1), 0).astype(F32)
    cc = cc_ref[...]
    ss = ss_ref[...]
    for h in range(RET_HEADS):
        lg = logg_ref[h]
        decay = jnp.where(diff >= 0, jnp.exp(lg * jnp.maximum(diff, 0.0)), 0.0)
        q_dec = jnp.exp(lg * (col + 1.0))
        k_dec = jnp.exp(lg * (n - 1.0 - col))
        g_chunk = jnp.exp(jnp.full((1, 1), lg * n, F32))
        ks = slice(h * RET_DK, (h + 1) * RET_DK)
        vs = slice(h * RET_DV, (h + 1) * RET_DV)
        q = _rope_full(rq_ref[:, ks], cc, ss)
        k = _rope_full(rk_ref[:, ks], cc, ss) * (RET_DK ** -0.5)
        v = rv_ref[:, vs].astype(BF16)
        qb = q.astype(BF16)
        sc = lax.dot_general(qb, k.astype(BF16), NT_DIMS, preferred_element_type=F32) * decay
        o_in = jnp.dot(sc.astype(BF16), v, preferred_element_type=F32)
        s_prev = s_scr[h]
        o_x = jnp.dot(qb, s_prev.astype(BF16), preferred_element_type=F32) * q_dec
        kd_t = (k * k_dec).T.astype(BF16)
        s_scr[h] = g_chunk * s_prev + jnp.dot(kd_t, v, preferred_element_type=F32)
        r = _group_norm_gate(o_in + o_x, gn_ref[:, vs], rg_ref[:, vs])
        o_ref[:, vs] = r.astype(o_ref.dtype)

    @pl.when(c == pl.num_programs(0) - 1)
    def _():
        st_ref[...] = s_scr[...]


def retention_prompt(z_ret, cc, ss, log_g, gn):
    s = z_ret.shape[0]
    n = RET_CHUNK
    qk_w = RET_HEADS * RET_DK
    v_w = RET_HEADS * RET_DV
    return pl.pallas_call(
        _ret_prompt_body,
        grid=(s // n,),
        in_specs=[pl.BlockSpec(memory_space=pltpu.SMEM),
                  pl.BlockSpec((n, qk_w), lambda c: (c, 0)),
                  pl.BlockSpec((n, qk_w), lambda c: (c, 1)),
                  pl.BlockSpec((n, v_w), lambda c: (c, 1)),
                  pl.BlockSpec((n, v_w), lambda c: (c, 2)),
                  pl.BlockSpec((n, LANES), lambda c: (c, 0)),
                  pl.BlockSpec((n, LANES), lambda c: (c, 0)),
                  pl.BlockSpec((1, v_w), lambda c: (0, 0))],
        out_specs=[pl.BlockSpec((n, v_w), lambda c: (c, 0)),
                   pl.BlockSpec((RET_HEADS, RET_DK, RET_DV), lambda c: (0, 0, 0))],
        out_shape=[jax.ShapeDtypeStruct((s, v_w), BF16),
                   jax.ShapeDtypeStruct((RET_HEADS, RET_DK, RET_DV), F32)],
        scratch_shapes=[pltpu.VMEM((RET_HEADS, RET_DK, RET_DV), F32)],
        compiler_params=_params("arbitrary"),
        name="retention_prompt",
    )(log_g, z_ret, z_ret, z_ret, z_ret, cc, ss, gn)


def _ret_sample_body(logg_ref, rq_ref, rk_ref, rv_ref, rg_ref, cc_ref, ss_ref, gn_ref, st_ref,
                     o_ref, so_ref, o_scr, *, tb):
    cc = cc_ref[...]
    ss = ss_ref[...]
    pad = jnp.zeros((LANES - tb, RET_DK), F32)
    for h in range(RET_HEADS):
        g1 = jnp.exp(jnp.full((1, 1), logg_ref[h], F32))
        ks = slice(h * RET_DK, (h + 1) * RET_DK)
        vs = slice(h * RET_DV, (h + 1) * RET_DV)
        q = _rope_full(rq_ref[:, ks], cc, ss)
        k = _rope_full(rk_ref[:, ks], cc, ss) * (RET_DK ** -0.5)
        q_t = jnp.concatenate([q, pad], axis=0).T
        k_t = jnp.concatenate([k, pad], axis=0).T
        v = rv_ref[:, vs]
        for b in range(tb):
            s_new = g1 * st_ref[b, h] + k_t[:, b:b + 1] * v[b:b + 1, :]
            so_ref[b, h] = s_new
            o_scr[b:b + 1, vs] = jnp.sum(q_t[:, b:b + 1] * s_new, axis=0, keepdims=True)
    for h in range(RET_HEADS):
        vs = slice(h * RET_DV, (h + 1) * RET_DV)
        r = _group_norm_gate(o_scr[:, vs], gn_ref[:, vs], rg_ref[:, vs])
        o_ref[:, vs] = r.astype(o_ref.dtype)


def retention_sample(z_ret, cc, ss, log_g, gn, state):
    b = z_ret.shape[0]
    tb = 8
    qk_w = RET_HEADS * RET_DK
    v_w = RET_HEADS * RET_DV
    st_block = (tb, RET_HEADS, RET_DK, RET_DV)
    return pl.pallas_call(
        functools.partial(_ret_sample_body, tb=tb),
        grid=(b // tb,),
        in_specs=[pl.BlockSpec(memory_space=pltpu.SMEM),
                  pl.BlockSpec((tb, qk_w), lambda i: (i, 0)),
                  pl.BlockSpec((tb, qk_w), lambda i: (i, 1)),
                  pl.BlockSpec((tb, v_w), lambda i: (i, 1)),
                  pl.BlockSpec((tb, v_w), lambda i: (i, 2)),
                  pl.BlockSpec((tb, LANES), lambda i: (i, 0)),
                  pl.BlockSpec((tb, LANES), lambda i: (i, 0)),
                  pl.BlockSpec((1, v_w), lambda i: (0, 0)),
                  pl.BlockSpec(st_block, lambda i: (i, 0, 0, 0))],
        out_specs=[pl.BlockSpec((tb, v_w), lambda i: (i, 0)),
                   pl.BlockSpec(st_block, lambda i: (i, 0, 0, 0))],
        out_shape=[jax.ShapeDtypeStruct((b, v_w), BF16),
                   jax.ShapeDtypeStruct(state.shape, F32)],
        scratch_shapes=[pltpu.VMEM((tb, v_w), F32)],
        compiler_params=_params("parallel"),
        name="retention_sample",
    )(log_g, z_ret, z_ret, z_ret, z_ret, cc, ss, gn, state)


def _pool_prompt_body(h_ref, halo_ref, g_ref, w_ref, sc_ref, o_ref, tail_ref, ext_scr, *, tm):
    i = pl.program_id(0)
    x = h_ref[...]
    g = g_ref[...]
    xn = (x * lax.rsqrt(jnp.mean(x * x, axis=-1, keepdims=True) + EPS)) * g
    xh = halo_ref[...]
    halo = (xh * lax.rsqrt(jnp.mean(xh * xh, axis=-1, keepdims=True) + EPS)) * g
    ext_scr[0:POOL_HALO, :] = jnp.where(i == 0, 0.0, halo)
    ext_scr[POOL_HALO:POOL_HALO + tm, :] = xn
    pos = i * tm + lax.broadcasted_iota(jnp.int32, (tm, 1), 0)
    ch = x.shape[1] // POOL_GROUPS
    for gi, win in enumerate(POOL_WINDOWS):
        cs = slice(gi * ch, (gi + 1) * ch)
        wsum = xn[:, cs]
        for back in range(1, win):
            wsum = wsum + ext_scr[POOL_HALO - back:POOL_HALO - back + tm, cs]
        cnt = jnp.minimum(win, pos + 1).astype(F32)
        pooled = wsum / cnt - xn[:, cs]
        y = jnp.dot(pooled.astype(BF16), w_ref[gi], preferred_element_type=F32)
        o_ref[:, cs] = x[:, cs] + y * sc_ref[:, cs]
    tail_ref[...] = ext_scr[tm:tm + POOL_HALO, :]


def pool_prompt(h, g, w, scale):
    s, d = h.shape
    tm = min(s, 256)
    ch = d // POOL_GROUPS
    halo_blocks = tm // POOL_HALO
    return pl.pallas_call(
        functools.partial(_pool_prompt_body, tm=tm),
        grid=(s // tm,),
        in_specs=[pl.BlockSpec((tm, d), lambda i: (i, 0)),
                  pl.BlockSpec((POOL_HALO, d), lambda i: (jnp.maximum(i * halo_blocks - 1, 0), 0)),
                  pl.BlockSpec((1, d), lambda i: (0, 0)),
                  pl.BlockSpec((POOL_GROUPS, ch, ch), lambda i: (0, 0, 0)),
                  pl.BlockSpec((1, d), lambda i: (0, 0))],
        out_specs=[pl.BlockSpec((tm, d), lambda i: (i, 0)),
                   pl.BlockSpec((POOL_HALO, d), lambda i: (0, 0))],
        out_shape=[jax.ShapeDtypeStruct((s, d), F32),
                   jax.ShapeDtypeStruct((POOL_HALO, d), F32)],
        scratch_shapes=[pltpu.VMEM((tm + POOL_HALO, d), F32)],
        compiler_params=_params("arbitrary"),
        name="pool_prompt",
    )(h, h, g, w, scale)


def _pool_sample_body(h_ref, st_ref, g_ref, w_ref, sc_ref, o_ref, buf_ref, *, count):
    x = h_ref[...]
    xn = (x * lax.rsqrt(jnp.mean(x * x, axis=-1, keepdims=True) + EPS)) * g_ref[...]
    ch = x.shape[1] // POOL_GROUPS
    for gi, win in enumerate(POOL_WINDOWS):
        cs = slice(gi * ch, (gi + 1) * ch)
        wsum = xn[:, cs]
        for back in range(1, win):
            wsum = wsum + st_ref[POOL_BUF - back, :, cs]
        pooled = wsum / float(count[gi]) - xn[:, cs]
        y = jnp.dot(pooled.astype(BF16), w_ref[gi], preferred_element_type=F32)
        o_ref[:, cs] = x[:, cs] + y * sc_ref[:, cs]
    for r in range(POOL_BUF - 1):
        buf_ref[r] = st_ref[r + 1]
    buf_ref[POOL_BUF - 1] = xn


def pool_sample(h, state_t, g, w, scale, pos):
    b, d = h.shape
    tb = min(b, 32)
    ch = d // POOL_GROUPS
    count = tuple(min(win, pos + 1) for win in POOL_WINDOWS)
    return pl.pallas_call(
        functools.partial(_pool_sample_body, count=count),
        grid=(b // tb,),
        in_specs=[pl.BlockSpec((tb, d), lambda i: (i, 0)),
                  pl.BlockSpec((POOL_BUF, tb, d), lambda i: (0, i, 0)),
                  pl.BlockSpec((1, d), lambda i: (0, 0)),
                  pl.BlockSpec((POOL_GROUPS, ch, ch), lambda i: (0, 0, 0)),
                  pl.BlockSpec((1, d), lambda i: (0, 0))],
        out_specs=[pl.BlockSpec((tb, d), lambda i: (i, 0)),
                   pl.BlockSpec((POOL_BUF, tb, d), lambda i: (0, i, 0))],
        out_shape=[jax.ShapeDtypeStruct((b, d), F32),
                   jax.ShapeDtypeStruct((POOL_BUF, b, d), F32)],
        compiler_params=_params("parallel"),
        name="pool_sample",
    )(h, state_t, g, w, scale)


def _rope_tables(pos, half, base):
    inv = jnp.power(base, -jnp.arange(half, dtype=F32) / half)
    ang = pos.astype(F32)[:, None] * inv[None, :]
    return jnp.cos(ang), jnp.sin(ang)


def _swap_layout(w):
    half = QK_ROPE // 2
    x1, x2 = w[..., :half], w[..., half:]
    return jnp.concatenate([x1, x2, x2, x1], axis=-1)


def _channel_and_ple(h, p_i, layer, g_mlp, w_up, w_down, g_ple, w_pe, w_pg):
    xn = rmsnorm_bf16(h, g_mlp)
    u = matmul(xn, w_up, layer, out_dtype=BF16, epilogue=_ep_relu2, name="mlp_up")
    h = matmul(u, w_down, layer, out_dtype=F32, epilogue=_ep_residual, extra=(h,),
               extra_specs=_residual_specs, name="mlp_down")
    xn = rmsnorm_bf16(h, g_ple)
    return matmul(xn, w_pg, layer, out_dtype=F32, epilogue=_ep_ple, extra=(h, p_i, w_pe),
                  extra_specs=_ple_specs(p_i.shape[1], layer), name="ple_gate")


def kernel(x_prompt, x_sample, cache_mla, state_ret, state_pool, page_table, p_prompt, p_sample,
           norm_mix, w_in, q_gain, kv_gain, kr_gain, w_uk, w_uv, ret_gn, w_out, pool_w, pool_scale,
           norm_mlp, w_up, w_down, norm_ple, w_pe, w_pg):
    assert x_prompt.shape[0] == 1 and x_sample.shape[1] == 1
    depth = norm_mix.shape[0]
    _, seq, d = x_prompt.shape
    nb = x_sample.shape[0]
    past_len = page_table.shape[1] * cache_mla.shape[2]
    hp = x_prompt.reshape(seq, d)
    hs = x_sample.reshape(nb, d)
    pos_p = jnp.arange(seq)
    pos_s = jnp.full((nb,), past_len, jnp.int32)
    log_g = jnp.log1p(-jnp.power(2.0, -5.0 - jnp.arange(RET_HEADS, dtype=F32)))
    cache_t = cache_mla.transpose(0, 1, 3, 2)

    tabs = []
    for pos in (pos_p, pos_s):
        c, s = _rope_tables(pos, QK_ROPE // 2, ROPE_BASE)
        tab_mla = jnp.concatenate([c, c, -s, s], axis=-1)
        c, s = _rope_tables(pos, RET_DK // 2, RET_ROT_BASE)
        tabs.append((tab_mla, jnp.concatenate([c, c], axis=-1), jnp.concatenate([-s, s], axis=-1)))

    wu = w_up.astype(BF16)
    wd = w_down.astype(BF16)
    wpe = w_pe.astype(BF16)
    wpg = w_pg.astype(BF16)
    q_w = MLA_HEADS * QK_HEAD
    mla_p, mla_s, ret_p, ret_s, pool_p, pool_s = [], [], [], [], [], []
    for i in range(depth):
        if i % 2 == 0:
            e = i // 2
            wq = w_in[e][:, :q_w].reshape(d, MLA_HEADS, QK_HEAD)
            wq = jnp.concatenate([wq[..., :QK_NOPE], _swap_layout(wq[..., QK_NOPE:])], axis=-1)
            wq = wq.reshape(d, MLA_HEADS // Q_HEADS_PER_STEP, -1).transpose(1, 0, 2).astype(BF16)
            gq = jnp.concatenate([q_gain[e][:QK_NOPE], _swap_layout(q_gain[e][QK_NOPE:])]).reshape(1, -1)
            wkv = w_in[e][:, q_w:q_w + MLA_ROW]
            wkv = jnp.concatenate([wkv[:, :KV_LORA], _swap_layout(wkv[:, KV_LORA:])], axis=-1).astype(BF16)
            gkv = kv_gain[e].reshape(1, -1)
            gkr = _swap_layout(kr_gain[e]).reshape(1, -1)
            w_ret = w_in[e][None, :, q_w + MLA_ROW:].astype(BF16)
            wuk_t = w_uk[e].transpose(1, 2, 0).astype(BF16)
            wuv_t = w_uv[e].transpose(1, 0, 2).astype(BF16)
            gn = ret_gn[e].reshape(1, -1)
            w_o = w_out[e].astype(BF16)

            def project(h, tab):
                xn = rmsnorm_bf16(h, norm_mix[i])
                qf = q_heads(xn, wq, gq, tab[0], wuk_t)
                rows, rows_bf = kv_rows(xn, wkv, gkv, gkr, tab[0])
                z_ret = matmul(xn, w_ret, 0, out_dtype=F32, name="ret_proj")
                return qf, rows, rows_bf, z_ret

            def output(h, v, r):
                return out_project(v, r, w_o, h)

            qf, rows, rows_bf, z_ret = project(hp, tabs[0])
            v = mla_prompt_attn(qf, rows_bf, wuv_t)
            r, s_fin = retention_prompt(z_ret, tabs[0][1], tabs[0][2], log_g, gn)
            hp = output(hp, v, r)
            mla_p.append(rows.reshape(1, seq, MLA_ROW))
            ret_p.append(s_fin[None])

            qf, rows, rows_bf, z_ret = project(hs, tabs[1])
            o_lat = mla_decode_attn(page_table, qf.transpose(1, 0, 2), rows.reshape(nb, 1, MLA_ROW),
                                    cache_t, e)
            v = uv_project(o_lat.transpose(1, 0, 2), wuv_t)
            r, s_new = retention_sample(z_ret, tabs[1][1], tabs[1][2], log_g, gn, state_ret[e])
            hs = output(hs, v, r)
            mla_s.append(rows.reshape(nb, 1, MLA_ROW))
            ret_s.append(s_new)
        else:
            o = i // 2
            g = norm_mix[i].reshape(1, -1)
            pw = pool_w[o].astype(BF16)
            sc = pool_scale[o].reshape(1, -1)
            hp, tail = pool_prompt(hp, g, pw, sc)
            pool_p.append(tail[POOL_HALO - POOL_BUF:][None])
            hs, buf = pool_sample(hs, state_pool[o].transpose(1, 0, 2), g, pw, sc, past_len)
            pool_s.append(buf.transpose(1, 0, 2))
        hp = _channel_and_ple(hp, p_prompt[i, 0], i, norm_mlp[i], wu, wd, norm_ple[i], wpe, wpg)
        hs = _channel_and_ple(hs, p_sample[i, :, 0], i, norm_mlp[i], wu, wd, norm_ple[i], wpe, wpg)

    return (hp.reshape(1, seq, d), hs.reshape(nb, 1, d),
            jnp.stack(mla_p), jnp.stack(mla_s), jnp.stack(ret_p), jnp.stack(ret_s),
            jnp.stack(pool_p), jnp.stack(pool_s))
```

```python
import functools

import jax
import jax.numpy as jnp
from jax import lax
from jax.experimental import pallas as pl
from jax.experimental.pallas import tpu as pltpu

F32 = jnp.float32
BF16 = jnp.bfloat16

EPS = 1e-6
NEG_INF = -1e30
MLA_HEADS = 16
QK_NOPE = 128
QK_ROPE = 64
QK_HEAD = QK_NOPE + QK_ROPE
V_HEAD = 128
KV_LORA = 512
MLA_ROW = KV_LORA + QK_ROPE
MLA_SCALE = QK_HEAD ** -0.5
ROPE_BASE = 10000.0
RET_HEADS = 8
RET_DK = 128
RET_DV = 256
RET_CHUNK = 128
RET_ROT_BASE = 10000.0
POOL_WINDOWS = (2, 4, 8, 16)
POOL_GROUPS = 4
POOL_BUF = 15
POOL_HALO = 16
Q_BLOCK = 128

LOG2E = 1.4426950408889634
LANES = 128
VMEM_LIMIT = 48 * 1024 * 1024

NT_DIMS = (((1,), (1,)), ((), ()))


def _params(*sem):
    return pltpu.CompilerParams(dimension_semantics=sem, vmem_limit_bytes=VMEM_LIMIT)


def _sigmoid(x):
    return 1.0 / (1.0 + jnp.exp(-x))


def _rms_scale(x, width):
    return lax.rsqrt(jnp.sum(x * x, axis=-1, keepdims=True) / width + EPS)


def _rmsnorm_body(x_ref, g_ref, o_ref):
    x = x_ref[...]
    r = lax.rsqrt(jnp.mean(x * x, axis=-1, keepdims=True) + EPS)
    o_ref[...] = ((x * r) * g_ref[...]).astype(o_ref.dtype)


def rmsnorm_bf16(x, g):
    m, d = x.shape
    tm = min(m, 512)
    return pl.pallas_call(
        _rmsnorm_body,
        grid=(m // tm,),
        in_specs=[pl.BlockSpec((tm, d), lambda i: (i, 0)),
                  pl.BlockSpec((1, d), lambda i: (0, 0))],
        out_specs=pl.BlockSpec((tm, d), lambda i: (i, 0)),
        out_shape=jax.ShapeDtypeStruct((m, d), BF16),
        compiler_params=_params("parallel"),
        name="rmsnorm",
    )(x, g.reshape(1, d))


def _ep_plain(acc):
    return acc


def _ep_relu2(acc):
    return jnp.square(jnp.maximum(acc, 0.0))


def _ep_residual(acc, res_ref):
    return res_ref[...] + acc


def _ep_ple(acc, h_ref, p_ref, wpe_ref):
    emb = jnp.dot(p_ref[...].astype(BF16), wpe_ref[...], preferred_element_type=F32)
    return h_ref[...] + emb * _sigmoid(acc)


def _mm_body(*refs, nk, n_extra, epilogue):
    a_ref, w_ref = refs[0], refs[1]
    extra = refs[2:2 + n_extra]
    o_ref = refs[2 + n_extra]
    if nk == 1:
        acc = jnp.dot(a_ref[...], w_ref[...], preferred_element_type=F32)
        o_ref[...] = epilogue(acc, *extra).astype(o_ref.dtype)
        return
    acc_ref = refs[3 + n_extra]
    k = pl.program_id(2)

    @pl.when(k == 0)
    def _():
        acc_ref[...] = jnp.zeros_like(acc_ref)

    acc_ref[...] += jnp.dot(a_ref[...], w_ref[...], preferred_element_type=F32)

    @pl.when(k == nk - 1)
    def _():
        o_ref[...] = epilogue(acc_ref[...], *extra).astype(o_ref.dtype)


def matmul(a, w, layer, *, out_dtype, epilogue=_ep_plain, extra=(), extra_specs=None, name="matmul"):
    m, kdim = a.shape
    n = w.shape[2]
    tm = min(m, 1024)
    tn = min(n, 1024)
    tk = min(kdim, 2048)
    nk = kdim // tk
    specs = [pl.BlockSpec((tm, tk), lambda i, j, k: (i, k)),
             pl.BlockSpec((None, tk, tn), lambda i, j, k: (layer, k, j))]
    if extra_specs is not None:
        specs += extra_specs(tm, tn)
    scratch = [pltpu.VMEM((tm, tn), F32)] if nk > 1 else []
    return pl.pallas_call(
        functools.partial(_mm_body, nk=nk, n_extra=len(extra), epilogue=epilogue),
        grid=(m // tm, n // tn, nk),
        in_specs=specs,
        out_specs=pl.BlockSpec((tm, tn), lambda i, j, k: (i, j)),
        out_shape=jax.ShapeDtypeStruct((m, n), out_dtype),
        scratch_shapes=scratch,
        compiler_params=_params("parallel", "parallel", "arbitrary"),
        name=name,
    )(a, w, *extra)


def _out_project_body(v_ref, r_ref, wv_ref, wr_ref, res_ref, o_ref):
    acc = jnp.dot(v_ref[...], wv_ref[...], preferred_element_type=F32)
    acc += jnp.dot(r_ref[...], wr_ref[...], preferred_element_type=F32)
    o_ref[...] = res_ref[...] + acc


def out_project(v, r, w, res):
    m, half = v.shape
    n = w.shape[1]
    tm = min(m, 512)
    tn = min(n, 1024)
    return pl.pallas_call(
        _out_project_body,
        grid=(m // tm, n // tn),
        in_specs=[pl.BlockSpec((tm, half), lambda i, j: (i, 0)),
                  pl.BlockSpec((tm, half), lambda i, j: (i, 0)),
                  pl.BlockSpec((half, tn), lambda i, j: (0, j)),
                  pl.BlockSpec((half, tn), lambda i, j: (1, j)),
                  pl.BlockSpec((tm, tn), lambda i, j: (i, j))],
        out_specs=pl.BlockSpec((tm, tn), lambda i, j: (i, j)),
        out_shape=jax.ShapeDtypeStruct((m, n), F32),
        compiler_params=_params("parallel", "parallel"),
        name="out_proj",
    )(v, r, w, w, res)


def _residual_specs(tm, tn):
    return [pl.BlockSpec((tm, tn), lambda i, j, k: (i, j))]


def _ple_specs(ple_dim, layer):
    def specs(tm, tn):
        return [pl.BlockSpec((tm, tn), lambda i, j, k: (i, j)),
                pl.BlockSpec((None, tm, ple_dim), lambda i, j, k: (layer, i, 0)),
                pl.BlockSpec((None, ple_dim, tn), lambda i, j, k: (layer, 0, j))]
    return specs


def _rotary_pairs(n1, tab):
    t = n1 * tab
    return t + pltpu.roll(t, 64, axis=1)


Q_HEADS_PER_STEP = 4


def _qheads_body(x_ref, w_ref, g_ref, tab_ref, wuk_ref, o_ref):
    x = x_ref[...]
    pair_w = 4 * LANES
    for hh in range(Q_HEADS_PER_STEP):
        if hh % 2 == 0:
            pair = hh // 2
            acc2 = jnp.dot(x, w_ref[:, pair * pair_w:(pair + 1) * pair_w],
                           preferred_element_type=F32)
        base = (hh % 2) * 2 * LANES
        a0 = acc2[:, base:base + QK_NOPE]
        a1 = acc2[:, base + QK_NOPE:base + 2 * LANES]
        lane = lax.broadcasted_iota(jnp.int32, a1.shape, 1)
        ssq = (jnp.sum(a0 * a0, axis=-1, keepdims=True)
               + jnp.sum(jnp.where(lane < QK_ROPE, a1 * a1, 0.0), axis=-1, keepdims=True))
        r = lax.rsqrt(ssq / QK_HEAD + EPS)
        n0 = (a0 * r) * g_ref[:, :QK_NOPE]
        n1 = (a1 * r) * g_ref[:, QK_NOPE:]
        pe = _rotary_pairs(n1, tab_ref[...])
        q_lat = jnp.dot(n0.astype(BF16), wuk_ref[hh], preferred_element_type=F32)
        o_ref[hh, :, :KV_LORA] = q_lat.astype(o_ref.dtype)
        o_ref[hh, :, KV_LORA:] = pe[:, :QK_ROPE].astype(o_ref.dtype)


def q_heads(xn, wq, gq, tab, wuk_t):
    m, d = xn.shape
    tm = min(m, 1024)
    hs = Q_HEADS_PER_STEP
    return pl.pallas_call(
        _qheads_body,
        grid=(m // tm, MLA_HEADS // hs),
        in_specs=[pl.BlockSpec((tm, d), lambda i, h: (i, 0)),
                  pl.BlockSpec((None, d, hs * 2 * LANES), lambda i, h: (h, 0, 0)),
                  pl.BlockSpec((1, 2 * LANES), lambda i, h: (0, 0)),
                  pl.BlockSpec((tm, LANES), lambda i, h: (i, 0)),
                  pl.BlockSpec((hs, QK_NOPE, KV_LORA), lambda i, h: (h, 0, 0))],
        out_specs=pl.BlockSpec((hs, tm, MLA_ROW), lambda i, h: (h, i, 0)),
        out_shape=jax.ShapeDtypeStruct((MLA_HEADS, m, MLA_ROW), BF16),
        compiler_params=_params("parallel", "arbitrary"),
        name="q_heads",
    )(xn, wq, gq, tab, wuk_t)


def _kvrows_body(x_ref, w_ref, gc_ref, gr_ref, tab_ref, rows_ref, rows_bf_ref):
    acc = jnp.dot(x_ref[...], w_ref[...], preferred_element_type=F32)
    ac = acc[:, :KV_LORA]
    c = (ac * _rms_scale(ac, KV_LORA)) * gc_ref[...]
    a1 = acc[:, KV_LORA:]
    lane = lax.broadcasted_iota(jnp.int32, a1.shape, 1)
    ssq = jnp.sum(jnp.where(lane < QK_ROPE, a1 * a1, 0.0), axis=-1, keepdims=True)
    n1 = (a1 * lax.rsqrt(ssq / QK_ROPE + EPS)) * gr_ref[...]
    kr = _rotary_pairs(n1, tab_ref[...])[:, :QK_ROPE]
    rows_ref[:, :KV_LORA] = c
    rows_ref[:, KV_LORA:] = kr
    rows_bf_ref[:, :KV_LORA] = c.astype(BF16)
    rows_bf_ref[:, KV_LORA:] = kr.astype(BF16)


def kv_rows(xn, wkv, gc, gr, tab):
    m, d = xn.shape
    tm = min(m, 512)
    width = KV_LORA + LANES
    return pl.pallas_call(
        _kvrows_body,
        grid=(m // tm,),
        in_specs=[pl.BlockSpec((tm, d), lambda i: (i, 0)),
                  pl.BlockSpec((d, width), lambda i: (0, 0)),
                  pl.BlockSpec((1, KV_LORA), lambda i: (0, 0)),
                  pl.BlockSpec((1, LANES), lambda i: (0, 0)),
                  pl.BlockSpec((tm, LANES), lambda i: (i, 0))],
        out_specs=[pl.BlockSpec((tm, MLA_ROW), lambda i: (i, 0)),
                   pl.BlockSpec((tm, MLA_ROW), lambda i: (i, 0))],
        out_shape=[jax.ShapeDtypeStruct((m, MLA_ROW), F32),
                   jax.ShapeDtypeStruct((m, MLA_ROW), BF16)],
        compiler_params=_params("parallel"),
        name="kv_rows",
    )(xn, wkv, gc, gr, tab)


def _attn_body(q_ref, k_ref, wuv_ref, o_ref, s_a, s_b, m_scr, l_scr, acc_scr, *, tq, tk):
    i = pl.program_id(0)
    rows = MLA_HEADS * tq
    c = MLA_SCALE * LOG2E
    m_scr[...] = jnp.full(m_scr.shape, NEG_INF, F32)
    l_scr[...] = jnp.zeros(l_scr.shape, F32)
    acc_scr[...] = jnp.zeros(acc_scr.shape, F32)

    def scores(j, s_ref):
        start = pl.multiple_of(j * tk, tk)
        q = q_ref[...].reshape(rows, MLA_ROW)
        s_ref[...] = lax.dot_general(q, k_ref[pl.ds(start, tk), :], NT_DIMS,
                                     preferred_element_type=F32)

    def update(j, s_ref, masked):
        start = pl.multiple_of(j * tk, tk)
        s = s_ref[...]
        if masked:
            qpos = i * tq + lax.broadcasted_iota(jnp.int32, (MLA_HEADS, tq, tk), 1).reshape(rows, tk)
            kpos = start + lax.broadcasted_iota(jnp.int32, (rows, tk), 1)
            s = jnp.where(kpos <= qpos, s, NEG_INF)
        m_prev = m_scr[...]
        m_new = jnp.maximum(m_prev, jnp.max(s, axis=-1, keepdims=True))
        alpha = jnp.exp2((m_prev - m_new) * c)
        p = jnp.exp2((s - m_new) * c)
        l_scr[...] = alpha * l_scr[...] + jnp.sum(p, axis=-1, keepdims=True)
        pv = jnp.dot(p.astype(BF16), k_ref[pl.ds(start, tk), :KV_LORA], preferred_element_type=F32)
        acc_scr[...] = alpha * acc_scr[...] + pv
        m_scr[...] = m_new

    n_full = (i * tq) // tk
    scores(0, s_a)

    def pair(t, carry):
        j = 2 * t
        scores(j + 1, s_b)
        update(j, s_a, False)
        scores(j + 2, s_a)
        update(j + 1, s_b, False)
        return carry

    lax.fori_loop(0, n_full // 2, pair, 0)
    odd = n_full % 2 == 1

    @pl.when(odd)
    def _():
        scores(n_full, s_b)
        update(n_full - 1, s_a, False)
        update(n_full, s_b, True)

    @pl.when(jnp.logical_not(odd))
    def _():
        update(n_full, s_a, True)

    for h in range(MLA_HEADS):
        rs = slice(h * tq, (h + 1) * tq)
        o_h = (acc_scr[rs, :] / l_scr[rs, :]).astype(BF16)
        v_h = jnp.dot(o_h, wuv_ref[h], preferred_element_type=F32)
        o_ref[:, h * V_HEAD:(h + 1) * V_HEAD] = v_h.astype(o_ref.dtype)


def mla_prompt_attn(qf, rows_bf, wuv_t):
    s = rows_bf.shape[0]
    tq = Q_BLOCK
    tk = min(s, 512)
    rows = MLA_HEADS * tq
    return pl.pallas_call(
        functools.partial(_attn_body, tq=tq, tk=tk),
        grid=(s // tq,),
        in_specs=[pl.BlockSpec((MLA_HEADS, tq, MLA_ROW), lambda i: (0, i, 0)),
                  pl.BlockSpec((s, MLA_ROW), lambda i: (0, 0), pipeline_mode=pl.Buffered(1)),
                  pl.BlockSpec((MLA_HEADS, KV_LORA, V_HEAD), lambda i: (0, 0, 0),
                               pipeline_mode=pl.Buffered(1))],
        out_specs=pl.BlockSpec((tq, MLA_HEADS * V_HEAD), lambda i: (i, 0)),
        out_shape=jax.ShapeDtypeStruct((s, MLA_HEADS * V_HEAD), BF16),
        scratch_shapes=[pltpu.VMEM((rows, tk), F32),
                        pltpu.VMEM((rows, tk), F32),
                        pltpu.VMEM((rows, 1), F32),
                        pltpu.VMEM((rows, 1), F32),
                        pltpu.VMEM((rows, KV_LORA), F32)],
        compiler_params=_params("parallel"),
        name="mla_prompt_attn",
    )(qf, rows_bf, wuv_t)


DECODE_PAGES = 32
DECODE_SLOTS = 3


def _decode_body(pt_ref, q_ref, new_ref, cache_ref, o_ref, pbuf, kbuf, m_scr, l_scr, acc_scr, sems,
                 *, e, n_pages, page, total):
    j = pl.program_id(1)
    n = pl.program_id(0) * pl.num_programs(1) + j

    def chunk_copies(chunk, slot):
        return [pltpu.make_async_copy(cache_ref.at[e, pt_ref[chunk * n_pages + i]],
                                      pbuf.at[slot, i], sems.at[slot])
                for i in range(n_pages)]

    def start(chunk, slot):
        for cp in chunk_copies(chunk, slot):
            cp.start()

    def wait(chunk, slot):
        for cp in chunk_copies(chunk, slot):
            cp.wait()

    last = total - 1

    @pl.when(n == 0)
    def _():
        start(0, 0)
        start(min(1, last), 1)

    slot = n % DECODE_SLOTS
    wait(n, slot)
    start(jnp.minimum(n + 2, last), (n + 2) % DECODE_SLOTS)

    q = q_ref[...]

    @pl.when(j == 0)
    def _():
        new = new_ref[...]
        s_new = jnp.sum(q.astype(F32) * new, axis=-1, keepdims=True) * MLA_SCALE
        m_scr[...] = s_new
        l_scr[...] = jnp.ones(l_scr.shape, F32)
        acc_scr[...] = jnp.broadcast_to(new[:, :KV_LORA], acc_scr.shape)

    for i in range(n_pages):
        kbuf[:, i * page:(i + 1) * page] = pbuf[slot, i].astype(BF16)
    s = jnp.dot(q, kbuf[...], preferred_element_type=F32) * MLA_SCALE
    m_prev = m_scr[...]
    m_new = jnp.maximum(m_prev, jnp.max(s, axis=-1, keepdims=True))
    alpha = jnp.exp(m_prev - m_new)
    p = jnp.exp(s - m_new)
    l_scr[...] = alpha * l_scr[...] + jnp.sum(p, axis=-1, keepdims=True)
    pv = lax.dot_general(p.astype(BF16), kbuf[:KV_LORA, :], NT_DIMS, preferred_element_type=F32)
    acc_scr[...] = alpha * acc_scr[...] + pv
    m_scr[...] = m_new

    @pl.when(j == pl.num_programs(1) - 1)
    def _():
        o_ref[...] = (acc_scr[...] / l_scr[...]).astype(o_ref.dtype)

    @pl.when(n == last)
    def _():
        wait(last, (n + 1) % DECODE_SLOTS)
        wait(last, (n + 2) % DECODE_SLOTS)


def mla_decode_attn(page_table, q_s, new_rows, cache_t, e):
    b, n_tab = page_table.shape
    page = cache_t.shape[3]
    n_pages = min(n_tab, DECODE_PAGES)
    steps = n_tab // n_pages
    grid_spec = pltpu.PrefetchScalarGridSpec(
        num_scalar_prefetch=1,
        grid=(b, steps),
        in_specs=[pl.BlockSpec((None, MLA_HEADS, MLA_ROW), lambda bi, j, pt: (bi, 0, 0)),
                  pl.BlockSpec((None, 1, MLA_ROW), lambda bi, j, pt: (bi, 0, 0)),
                  pl.BlockSpec(memory_space=pl.ANY)],
        out_specs=pl.BlockSpec((None, MLA_HEADS, KV_LORA), lambda bi, j, pt: (bi, 0, 0)),
        scratch_shapes=[pltpu.VMEM((DECODE_SLOTS, n_pages, MLA_ROW, page), F32),
                        pltpu.VMEM((MLA_ROW, n_pages * page), BF16),
                        pltpu.VMEM((MLA_HEADS, 1), F32),
                        pltpu.VMEM((MLA_HEADS, 1), F32),
                        pltpu.VMEM((MLA_HEADS, KV_LORA), F32),
                        pltpu.SemaphoreType.DMA((DECODE_SLOTS,))],
    )
    return pl.pallas_call(
        functools.partial(_decode_body, e=e, n_pages=n_pages, page=page, total=b * steps),
        grid_spec=grid_spec,
        out_shape=jax.ShapeDtypeStruct((b, MLA_HEADS, KV_LORA), BF16),
        compiler_params=_params("arbitrary", "arbitrary"),
        name="mla_decode_attn",
    )(page_table.reshape(-1), q_s, new_rows, cache_t)


def _uv_body(o_ref, w_ref, v_ref):
    v_ref[...] = jnp.dot(o_ref[...], w_ref[...], preferred_element_type=F32).astype(v_ref.dtype)


def uv_project(o_lat_t, wuv_t):
    _, b, _ = o_lat_t.shape
    return pl.pallas_call(
        _uv_body,
        grid=(MLA_HEADS,),
        in_specs=[pl.BlockSpec((None, b, KV_LORA), lambda h: (h, 0, 0)),
                  pl.BlockSpec((None, KV_LORA, V_HEAD), lambda h: (h, 0, 0))],
        out_specs=pl.BlockSpec((b, V_HEAD), lambda h: (0, h)),
        out_shape=jax.ShapeDtypeStruct((b, MLA_HEADS * V_HEAD), BF16),
        compiler_params=_params("parallel"),
        name="uv_project",
    )(o_lat_t, wuv_t)


def _rope_full(x, cc, ss):
    return x * cc + pltpu.roll(x, RET_DK // 2, axis=1) * ss


def _group_norm_gate(ro, gn, gate):
    mu = jnp.mean(ro, axis=-1, keepdims=True)
    d = ro - mu
    var = jnp.mean(d * d, axis=-1, keepdims=True)
    r = (d * lax.rsqrt(var + EPS)) * gn
    return (gate * _sigmoid(gate)) * r


def _ret_prompt_body(logg_ref, rq_ref, rk_ref, rv_ref, rg_ref, cc_ref, ss_ref, gn_ref,
                     o_ref, st_ref, s_scr):
    c = pl.program_id(0)
    n = RET_CHUNK

    @pl.when(c == 0)
    def _():
        s_scr[...] = jnp.zeros(s_scr.shape, F32)

    ii = lax.broadcasted_iota(jnp.int32, (n, n), 0)
    jj = lax.broadcasted_iota(jnp.int32, (n, n), 1)
    diff = (ii - jj).astype(F32)
    col = lax.broadcasted_iota(jnp.int32, (n, 1), 0).astype(F32)
    cc = cc_ref[...]
    ss = ss_ref[...]
    for h in range(RET_HEADS):
        lg = logg_ref[h]
        decay = jnp.where(diff >= 0, jnp.exp(lg * jnp.maximum(diff, 0.0)), 0.0)
        q_dec = jnp.exp(lg * (col + 1.0))
        k_dec = jnp.exp(lg * (n - 1.0 - col))
        g_chunk = jnp.exp(jnp.full((1, 1), lg * n, F32))
        ks = slice(h * RET_DK, (h + 1) * RET_DK)
        vs = slice(h * RET_DV, (h + 1) * RET_DV)
        q = _rope_full(rq_ref[:, ks], cc, ss)
        k = _rope_full(rk_ref[:, ks], cc, ss) * (RET_DK ** -0.5)
        v = rv_ref[:, vs].astype(BF16)
        qb = q.astype(BF16)
        sc = lax.dot_general(qb, k.astype(BF16), NT_DIMS, preferred_element_type=F32) * decay
        o_in = jnp.dot(sc.astype(BF16), v, preferred_element_type=F32)
        s_prev = s_scr[h]
        o_x = jnp.dot(qb, s_prev.astype(BF16), preferred_element_type=F32) * q_dec
        kd_t = (k * k_dec).T.astype(BF16)
        s_scr[h] = g_chunk * s_prev + jnp.dot(kd_t, v, preferred_element_type=F32)
        r = _group_norm_gate(o_in + o_x, gn_ref[:, vs], rg_ref[:, vs])
        o_ref[:, vs] = r.astype(o_ref.dtype)

    @pl.when(c == pl.num_programs(0) - 1)
    def _():
        st_ref[...] = s_scr[...]


def retention_prompt(z_ret, cc, ss, log_g, gn):
    s = z_ret.shape[0]
    n = RET_CHUNK
    qk_w = RET_HEADS * RET_DK
    v_w = RET_HEADS * RET_DV
    return pl.pallas_call(
        _ret_prompt_body,
        grid=(s // n,),
        in_specs=[pl.BlockSpec(memory_space=pltpu.SMEM),
                  pl.BlockSpec((n, qk_w), lambda c: (c, 0)),
                  pl.BlockSpec((n, qk_w), lambda c: (c, 1)),
                  pl.BlockSpec((n, v_w), lambda c: (c, 1)),
                  pl.BlockSpec((n, v_w), lambda c: (c, 2)),
                  pl.BlockSpec((n, LANES), lambda c: (c, 0)),
                  pl.BlockSpec((n, LANES), lambda c: (c, 0)),
                  pl.BlockSpec((1, v_w), lambda c: (0, 0))],
        out_specs=[pl.BlockSpec((n, v_w), lambda c: (c, 0)),
                   pl.BlockSpec((RET_HEADS, RET_DK, RET_DV), lambda c: (0, 0, 0))],
        out_shape=[jax.ShapeDtypeStruct((s, v_w), BF16),
                   jax.ShapeDtypeStruct((RET_HEADS, RET_DK, RET_DV), F32)],
        scratch_shapes=[pltpu.VMEM((RET_HEADS, RET_DK, RET_DV), F32)],
        compiler_params=_params("arbitrary"),
        name="retention_prompt",
    )(log_g, z_ret, z_ret, z_ret, z_ret, cc, ss, gn)


def _ret_sample_body(logg_ref, rq_ref, rk_ref, rv_ref, rg_ref, cc_ref, ss_ref, gn_ref, st_ref,
                     o_ref, so_ref, o_scr, *, tb):
    cc = cc_ref[...]
    ss = ss_ref[...]
    pad = jnp.zeros((LANES - tb, RET_DK), F32)
    for h in range(RET_HEADS):
        g1 = jnp.exp(jnp.full((1, 1), logg_ref[h], F32))
        ks = slice(h * RET_DK, (h + 1) * RET_DK)
        vs = slice(h * RET_DV, (h + 1) * RET_DV)
        q = _rope_full(rq_ref[:, ks], cc, ss)
        k = _rope_full(rk_ref[:, ks], cc, ss) * (RET_DK ** -0.5)
        q_t = jnp.concatenate([q, pad], axis=0).T
        k_t = jnp.concatenate([k, pad], axis=0).T
        v = rv_ref[:, vs]
        for b in range(tb):
            s_new = g1 * st_ref[b, h] + k_t[:, b:b + 1] * v[b:b + 1, :]
            so_ref[b, h] = s_new
            o_scr[b:b + 1, vs] = jnp.sum(q_t[:, b:b + 1] * s_new, axis=0, keepdims=True)
    for h in range(RET_HEADS):
        vs = slice(h * RET_DV, (h + 1) * RET_DV)
        r = _group_norm_gate(o_scr[:, vs], gn_ref[:, vs], rg_ref[:, vs])
        o_ref[:, vs] = r.astype(o_ref.dtype)


def retention_sample(z_ret, cc, ss, log_g, gn, state):
    b = z_ret.shape[0]
    tb = 8
    qk_w = RET_HEADS * RET_DK
    v_w = RET_HEADS * RET_DV
    st_block = (tb, RET_HEADS, RET_DK, RET_DV)
    return pl.pallas_call(
        functools.partial(_ret_sample_body, tb=tb),
        grid=(b // tb,),
        in_specs=[pl.BlockSpec(memory_space=pltpu.SMEM),
                  pl.BlockSpec((tb, qk_w), lambda i: (i, 0)),
                  pl.BlockSpec((tb, qk_w), lambda i: (i, 1)),
                  pl.BlockSpec((tb, v_w), lambda i: (i, 1)),
                  pl.BlockSpec((tb, v_w), lambda i: (i, 2)),
                  pl.BlockSpec((tb, LANES), lambda i: (i, 0)),
                  pl.BlockSpec((tb, LANES), lambda i: (i, 0)),
                  pl.BlockSpec((1, v_w), lambda i: (0, 0)),
                  pl.BlockSpec(st_block, lambda i: (i, 0, 0, 0))],
        out_specs=[pl.BlockSpec((tb, v_w), lambda i: (i, 0)),
                   pl.BlockSpec(st_block, lambda i: (i, 0, 0, 0))],
        out_shape=[jax.ShapeDtypeStruct((b, v_w), BF16),
                   jax.ShapeDtypeStruct(state.shape, F32)],
        scratch_shapes=[pltpu.VMEM((tb, v_w), F32)],
        compiler_params=_params("parallel"),
        name="retention_sample",
    )(log_g, z_ret, z_ret, z_ret, z_ret, cc, ss, gn, state)


def _pool_prompt_body(h_ref, halo_ref, g_ref, w_ref, sc_ref, gn_ref, o_ref, tail_ref, xn_ref, ext_scr,
                      *, tm):
    i = pl.program_id(0)
    x = h_ref[...]
    g = g_ref[...]
    xn = (x * lax.rsqrt(jnp.mean(x * x, axis=-1, keepdims=True) + EPS)) * g
    xh = halo_ref[...]
    halo = (xh * lax.rsqrt(jnp.mean(xh * xh, axis=-1, keepdims=True) + EPS)) * g
    ext_scr[0:POOL_HALO, :] = jnp.where(i == 0, 0.0, halo)
    ext_scr[POOL_HALO:POOL_HALO + tm, :] = xn
    pos = i * tm + lax.broadcasted_iota(jnp.int32, (tm, 1), 0)
    ch = x.shape[1] // POOL_GROUPS
    for gi, win in enumerate(POOL_WINDOWS):
        cs = slice(gi * ch, (gi + 1) * ch)
        wsum = xn[:, cs]
        for back in range(1, win):
            wsum = wsum + ext_scr[POOL_HALO - back:POOL_HALO - back + tm, cs]
        cnt = jnp.minimum(win, pos + 1).astype(F32)
        pooled = wsum / cnt - xn[:, cs]
        y = jnp.dot(pooled.astype(BF16), w_ref[gi], preferred_element_type=F32)
        o_ref[:, cs] = x[:, cs] + y * sc_ref[:, cs]
    tail_ref[...] = ext_scr[tm:tm + POOL_HALO, :]
    hn = o_ref[...]
    r = lax.rsqrt(jnp.mean(hn * hn, axis=-1, keepdims=True) + EPS)
    xn_ref[...] = ((hn * r) * gn_ref[...]).astype(xn_ref.dtype)


def pool_prompt(h, g, w, scale, g_next):
    s, d = h.shape
    tm = min(s, 256)
    ch = d // POOL_GROUPS
    halo_blocks = tm // POOL_HALO
    return pl.pallas_call(
        functools.partial(_pool_prompt_body, tm=tm),
        grid=(s // tm,),
        in_specs=[pl.BlockSpec((tm, d), lambda i: (i, 0)),
                  pl.BlockSpec((POOL_HALO, d), lambda i: (jnp.maximum(i * halo_blocks - 1, 0), 0)),
                  pl.BlockSpec((1, d), lambda i: (0, 0)),
                  pl.BlockSpec((POOL_GROUPS, ch, ch), lambda i: (0, 0, 0)),
                  pl.BlockSpec((1, d), lambda i: (0, 0)),
                  pl.BlockSpec((1, d), lambda i: (0, 0))],
        out_specs=[pl.BlockSpec((tm, d), lambda i: (i, 0)),
                   pl.BlockSpec((POOL_HALO, d), lambda i: (0, 0)),
                   pl.BlockSpec((tm, d), lambda i: (i, 0))],
        out_shape=[jax.ShapeDtypeStruct((s, d), F32),
                   jax.ShapeDtypeStruct((POOL_HALO, d), F32),
                   jax.ShapeDtypeStruct((s, d), BF16)],
        scratch_shapes=[pltpu.VMEM((tm + POOL_HALO, d), F32)],
        compiler_params=_params("arbitrary"),
        name="pool_prompt",
    )(h, h, g, w, scale, g_next.reshape(1, d))


def _pool_sample_body(h_ref, st_ref, g_ref, w_ref, sc_ref, o_ref, buf_ref, *, count):
    x = h_ref[...]
    xn = (x * lax.rsqrt(jnp.mean(x * x, axis=-1, keepdims=True) + EPS)) * g_ref[...]
    ch = x.shape[1] // POOL_GROUPS
    for gi, win in enumerate(POOL_WINDOWS):
        cs = slice(gi * ch, (gi + 1) * ch)
        wsum = xn[:, cs]
        for back in range(1, win):
            wsum = wsum + st_ref[POOL_BUF - back, :, cs]
        pooled = wsum / float(count[gi]) - xn[:, cs]
        y = jnp.dot(pooled.astype(BF16), w_ref[gi], preferred_element_type=F32)
        o_ref[:, cs] = x[:, cs] + y * sc_ref[:, cs]
    for r in range(POOL_BUF - 1):
        buf_ref[r] = st_ref[r + 1]
    buf_ref[POOL_BUF - 1] = xn


def pool_sample(h, state_t, g, w, scale, pos):
    b, d = h.shape
    tb = min(b, 32)
    ch = d // POOL_GROUPS
    count = tuple(min(win, pos + 1) for win in POOL_WINDOWS)
    return pl.pallas_call(
        functools.partial(_pool_sample_body, count=count),
        grid=(b // tb,),
        in_specs=[pl.BlockSpec((tb, d), lambda i: (i, 0)),
                  pl.BlockSpec((POOL_BUF, tb, d), lambda i: (0, i, 0)),
                  pl.BlockSpec((1, d), lambda i: (0, 0)),
                  pl.BlockSpec((POOL_GROUPS, ch, ch), lambda i: (0, 0, 0)),
                  pl.BlockSpec((1, d), lambda i: (0, 0))],
        out_specs=[pl.BlockSpec((tb, d), lambda i: (i, 0)),
                   pl.BlockSpec((POOL_BUF, tb, d), lambda i: (0, i, 0))],
        out_shape=[jax.ShapeDtypeStruct((b, d), F32),
                   jax.ShapeDtypeStruct((POOL_BUF, b, d), F32)],
        compiler_params=_params("parallel"),
        name="pool_sample",
    )(h, state_t, g, w, scale)


def _rope_tables(pos, half, base):
    inv = jnp.power(base, -jnp.arange(half, dtype=F32) / half)
    ang = pos.astype(F32)[:, None] * inv[None, :]
    return jnp.cos(ang), jnp.sin(ang)


def _swap_layout(w):
    half = QK_ROPE // 2
    x1, x2 = w[..., :half], w[..., half:]
    return jnp.concatenate([x1, x2, x2, x1], axis=-1)


def _channel_and_ple(h, p, layer, g_mlp, w_up, w_down, g_ple, w_pe, w_pg, xn=None):
    if xn is None:
        xn = rmsnorm_bf16(h, g_mlp)
    u = matmul(xn, w_up, layer, out_dtype=BF16, epilogue=_ep_relu2, name="mlp_up")
    h = matmul(u, w_down, layer, out_dtype=F32, epilogue=_ep_residual, extra=(h,),
               extra_specs=_residual_specs, name="mlp_down")
    xn = rmsnorm_bf16(h, g_ple)
    return matmul(xn, w_pg, layer, out_dtype=F32, epilogue=_ep_ple, extra=(h, p, w_pe),
                  extra_specs=_ple_specs(p.shape[2], layer), name="ple_gate")


def kernel(x_prompt, x_sample, cache_mla, state_ret, state_pool, page_table, p_prompt, p_sample,
           norm_mix, w_in, q_gain, kv_gain, kr_gain, w_uk, w_uv, ret_gn, w_out, pool_w, pool_scale,
           norm_mlp, w_up, w_down, norm_ple, w_pe, w_pg):
    assert x_prompt.shape[0] == 1 and x_sample.shape[1] == 1
    depth = norm_mix.shape[0]
    _, seq, d = x_prompt.shape
    nb = x_sample.shape[0]
    past_len = page_table.shape[1] * cache_mla.shape[2]
    hp = x_prompt.reshape(seq, d)
    hs = x_sample.reshape(nb, d)
    pos_p = jnp.arange(seq)
    pos_s = jnp.full((nb,), past_len, jnp.int32)
    log_g = jnp.log1p(-jnp.power(2.0, -5.0 - jnp.arange(RET_HEADS, dtype=F32)))
    cache_t = cache_mla.transpose(0, 1, 3, 2)

    tabs = []
    for pos in (pos_p, pos_s):
        c, s = _rope_tables(pos, QK_ROPE // 2, ROPE_BASE)
        tab_mla = jnp.concatenate([c, c, -s, s], axis=-1)
        c, s = _rope_tables(pos, RET_DK // 2, RET_ROT_BASE)
        tabs.append((tab_mla, jnp.concatenate([c, c], axis=-1), jnp.concatenate([-s, s], axis=-1)))

    wu = w_up.astype(BF16)
    wd = w_down.astype(BF16)
    wpe = w_pe.astype(BF16)
    wpg = w_pg.astype(BF16)
    q_w = MLA_HEADS * QK_HEAD
    mla_p, mla_s, ret_p, ret_s, pool_p, pool_s = [], [], [], [], [], []
    pp = p_prompt.reshape(depth, seq, -1)
    ps = p_sample.reshape(depth, nb, -1)
    for i in range(depth):
        xn_p = None
        if i % 2 == 0:
            e = i // 2
            wq =w_in[e][:, :q_w].reshape(d, MLA_HEADS, QK_HEAD)
            wq = jnp.concatenate([wq[..., :QK_NOPE], _swap_layout(wq[..., QK_NOPE:])], axis=-1)
            wq = wq.reshape(d, MLA_HEADS // Q_HEADS_PER_STEP, -1).transpose(1, 0, 2).astype(BF16)
            gq = jnp.concatenate([q_gain[e][:QK_NOPE], _swap_layout(q_gain[e][QK_NOPE:])]).reshape(1, -1)
            wkv = w_in[e][:, q_w:q_w + MLA_ROW]
            wkv = jnp.concatenate([wkv[:, :KV_LORA], _swap_layout(wkv[:, KV_LORA:])], axis=-1).astype(BF16)
            gkv = kv_gain[e].reshape(1, -1)
            gkr = _swap_layout(kr_gain[e]).reshape(1, -1)
            w_ret = w_in[e][None, :, q_w + MLA_ROW:].astype(BF16)
            wuk_t = w_uk[e].transpose(1, 2, 0).astype(BF16)
            wuv_t = w_uv[e].transpose(1, 0, 2).astype(BF16)
            gn = ret_gn[e].reshape(1, -1)
            w_o = w_out[e].astype(BF16)

            def project(h, tab):
                xn = rmsnorm_bf16(h, norm_mix[i])
                qf = q_heads(xn, wq, gq, tab[0], wuk_t)
                rows, rows_bf = kv_rows(xn, wkv, gkv, gkr, tab[0])
                z_ret = matmul(xn, w_ret, 0, out_dtype=F32, name="ret_proj")
                return qf, rows, rows_bf, z_ret

            def output(h, v, r):
                return out_project(v, r, w_o, h)

            qf, rows, rows_bf, z_ret = project(hp, tabs[0])
            v = mla_prompt_attn(qf, rows_bf, wuv_t)
            r, s_fin = retention_prompt(z_ret, tabs[0][1], tabs[0][2], log_g, gn)
            hp = output(hp, v, r)
            mla_p.append(rows.reshape(1, seq, MLA_ROW))
            ret_p.append(s_fin[None])

            qf, rows, rows_bf, z_ret = project(hs, tabs[1])
            o_lat = mla_decode_attn(page_table, qf.transpose(1, 0, 2), rows.reshape(nb, 1, MLA_ROW),
                                    cache_t, e)
            v = uv_project(o_lat.transpose(1, 0, 2), wuv_t)
            r, s_new = retention_sample(z_ret, tabs[1][1], tabs[1][2], log_g, gn, state_ret[e])
            hs = output(hs, v, r)
            mla_s.append(rows.reshape(nb, 1, MLA_ROW))
            ret_s.append(s_new)
        else:
            o = i // 2
            g = norm_mix[i].reshape(1, -1)
            pw = pool_w[o].astype(BF16)
            sc = pool_scale[o].reshape(1, -1)
            hp, tail, xn_p = pool_prompt(hp, g, pw, sc, norm_mlp[i])
            pool_p.append(tail[POOL_HALO - POOL_BUF:][None])
            hs, buf = pool_sample(hs, state_pool[o].transpose(1, 0, 2), g, pw, sc, past_len)
            pool_s.append(buf.transpose(1, 0, 2))
        hp = _channel_and_ple(hp, pp, i, norm_mlp[i], wu, wd, norm_ple[i], wpe, wpg, xn=xn_p)
        hs = _channel_and_ple(hs, ps, i, norm_mlp[i], wu, wd, norm_ple[i], wpe, wpg)

    return (hp.reshape(1, seq, d), hs.reshape(nb, 1, d),
            jnp.stack(mla_p), jnp.stack(mla_s), jnp.stack(ret_p), jnp.stack(ret_s),
            jnp.stack(pool_p), jnp.stack(pool_s))
```
